```python
import math, functools
import jax, jax.numpy as jnp
from jax import lax
import numpy as np

D_MODEL = 2048
BATCH = 16
SEQ = 2048
DEPTH = 1
DEC_BATCH = 32
DEC_SEQ = 8
PAST_LEN = 16384
PAGE_SIZE = 128

HEAD_DIM = 128
N_MOBA_HEADS = 8
N_DN_HEADS = 8
MOBA_WIDTH = N_MOBA_HEADS * HEAD_DIM
DN_WIDTH = N_DN_HEADS * HEAD_DIM
MOBA_BLOCK = 256
MOBA_TOPK = 3
MOBA_Q_CHUNK = 16
DN_CHUNK = 64
CONV_W = 4
D_FF = 4 * D_MODEL
ROPE_THETA = 10000.0
NORM_EPS = 1e-6
ATTN_SCALE = HEAD_DIM ** -0.5
PROJ_DIM = 3 * MOBA_WIDTH + 4 * DN_WIDTH + 2 * N_DN_HEADS

kernel_name = "hymba_moba_gated_deltanet_step"

F32 = jnp.float32


def rmsnorm(x, w):
    xf = x.astype(F32)
    y = xf * lax.rsqrt(jnp.mean(xf * xf, axis=-1, keepdims=True) + NORM_EPS)
    return (y * w.astype(F32)).astype(x.dtype)


def l2norm(x):
    xf = x.astype(F32)
    return xf * lax.rsqrt(jnp.sum(xf * xf, axis=-1, keepdims=True) + NORM_EPS)


def rope(x, pos):
    half = HEAD_DIM // 2
    inv_freq = 1.0 / (ROPE_THETA ** (jnp.arange(half, dtype=F32) / half))
    ang = pos.astype(F32)[:, None] * inv_freq[None, :]
    cos = jnp.cos(ang)[None, :, None, :]
    sin = jnp.sin(ang)[None, :, None, :]
    xf = x.astype(F32)
    x1, x2 = xf[..., :half], xf[..., half:]
    return jnp.concatenate([x1 * cos - x2 * sin, x2 * cos + x1 * sin], axis=-1).astype(x.dtype)


def causal_conv(x, buf, w):
    L = x.shape[1]
    xp = jnp.concatenate([buf.astype(x.dtype), x], axis=1)
    y = xp[:, 0:L] * w[0]
    for j in range(1, CONV_W):
        y = y + xp[:, j:j + L] * w[j]
    return jax.nn.silu(y), xp[:, L:]


def gated_delta_chunked(q, k, v, g, beta, state0):
    B, L, H, DK = q.shape
    C = min(DN_CHUNK, L)
    pad = (-L) % C

    def prep(t):
        t = jnp.moveaxis(t.astype(F32), 1, 2)
        return jnp.pad(t, [(0, 0), (0, 0), (0, pad)] + [(0, 0)] * (t.ndim - 3))

    q, k, v, g, beta = prep(q), prep(k), prep(v), prep(g), prep(beta)
    n = (L + pad) // C
    q = q * (DK ** -0.5)
    q = q.reshape(B, H, n, C, -1)
    k = k.reshape(B, H, n, C, -1)
    v = v.reshape(B, H, n, C, -1)
    g = g.reshape(B, H, n, C)
    beta = beta.reshape(B, H, n, C)
    gc = jnp.cumsum(g, axis=-1)
    causal = jnp.tril(jnp.ones((C, C), dtype=bool))
    strict = jnp.tril(jnp.ones((C, C), dtype=bool), -1)
    decay = jnp.where(causal, jnp.exp(jnp.where(causal, gc[..., :, None] - gc[..., None, :], 0.0)), 0.0)
    kb = k * beta[..., None]
    vb = v * beta[..., None]
    a_kk = jnp.where(strict, jnp.einsum('bhnik,bhnjk->bhnij', kb, k) * decay, 0.0)
    eye = jnp.eye(C, dtype=F32)
    t_inv = lax.linalg.triangular_solve(a_kk + eye, jnp.broadcast_to(eye, a_kk.shape),
                                        left_side=True, lower=True, unit_diagonal=True)
    u = jnp.einsum('bhnij,bhnjv->bhniv', t_inv, vb)
    w = jnp.einsum('bhnij,bhnjk->bhnik', t_inv, kb * jnp.exp(gc)[..., None])
    a_qk = jnp.where(causal, jnp.einsum('bhnik,bhnjk->bhnij', q, k) * decay, 0.0)
    q_dec = q * jnp.exp(gc)[..., None]
    k_dec = k * jnp.exp(gc[..., -1:] - gc)[..., None]
    g_tot = jnp.exp(gc[..., -1])
    xs = tuple(jnp.moveaxis(t, 2, 0) for t in (u, w, q_dec, a_qk, k_dec, g_tot))

    def step(S, inp):
        u_i, w_i, qd_i, a_i, kd_i, gt_i = inp
        v_new = u_i - jnp.einsum('bhck,bhkv->bhcv', w_i, S)
        o_i = jnp.einsum('bhck,bhkv->bhcv', qd_i, S) + jnp.einsum('bhcj,bhjv->bhcv', a_i, v_new)
        S = S * gt_i[..., None, None] + jnp.einsum('bhck,bhcv->bhkv', kd_i, v_new)
        return S, o_i

    S, o = lax.scan(step, state0.astype(F32), xs)
    o = jnp.moveaxis(o, 0, 2).reshape(B, H, n * C, -1)[:, :, :L]
    return jnp.moveaxis(o, 1, 2), S


def gated_deltanet(qkv, z, b, a, conv_buf, state0, conv_w, a_log, dt_bias, norm_w):
    B, L, _ = qkv.shape
    qkv, new_buf = causal_conv(qkv, conv_buf, conv_w)
    q, k, v = jnp.split(qkv, 3, axis=-1)
    q = l2norm(q.reshape(B, L, N_DN_HEADS, HEAD_DIM))
    k = l2norm(k.reshape(B, L, N_DN_HEADS, HEAD_DIM))
    v = v.reshape(B, L, N_DN_HEADS, HEAD_DIM)
    beta = jax.nn.sigmoid(b.astype(F32))
    g = -jnp.exp(a_log.astype(F32)) * jax.nn.softplus(a.astype(F32) + dt_bias.astype(F32))
    o, S = gated_delta_chunked(q, k, v, g, beta, state0)
    o = rmsnorm(o, norm_w) * jax.nn.silu(z.reshape(B, L, N_DN_HEADS, HEAD_DIM).astype(F32))
    return o.reshape(B, L, DN_WIDTH).astype(qkv.dtype), new_buf, S.astype(state0.dtype)


def moba_core(qc, qpos, qb, kmean, fetch, own_k, own_v, own_pos):
    B, H, Q, D = qc.shape
    nb = kmean.shape[2]
    k_sel = min(MOBA_TOPK, nb)
    gate = jnp.einsum('bhqd,bhnd->bhqn', qc.astype(F32), kmean.astype(F32))
    gate = jnp.where(jnp.arange(nb) < qb, gate, -jnp.inf)
    _, idx = lax.top_k(gate, k_sel)
    sel_ok = idx < qb
    ks, vs = fetch(idx)
    s_sel = jnp.einsum('bhqd,bhqntd->bhqnt', qc, ks, preferred_element_type=F32) * ATTN_SCALE
    s_sel = jnp.where(sel_ok[..., None], s_sel, -jnp.inf).reshape(B, H, Q, k_sel * MOBA_BLOCK)
    s_own = jnp.einsum('bhqd,bhtd->bhqt', qc, own_k, preferred_element_type=F32) * ATTN_SCALE
    s_own = jnp.where(own_pos[None, :] <= qpos[:, None], s_own, -jnp.inf)
    p = jax.nn.softmax(jnp.concatenate([s_sel, s_own], axis=-1), axis=-1)
    p_sel = p[..., :k_sel * MOBA_BLOCK].reshape(B, H, Q, k_sel, MOBA_BLOCK)
    p_own = p[..., k_sel * MOBA_BLOCK:]
    out = (jnp.einsum('bhqnt,bhqntd->bhqd', p_sel, vs.astype(F32))
           + jnp.einsum('bhqt,bhtd->bhqd', p_own, own_v.astype(F32)))
    return out.astype(qc.dtype)


def moba_prompt(q, k, v):
    B, S, H, D = q.shape
    nb = -(-S // MOBA_BLOCK)
    pad = nb * MOBA_BLOCK - S
    qh = jnp.transpose(q, (0, 2, 1, 3))
    kh = jnp.pad(jnp.transpose(k, (0, 2, 1, 3)), ((0, 0), (0, 0), (0, pad), (0, 0)))
    vh = jnp.pad(jnp.transpose(v, (0, 2, 1, 3)), ((0, 0), (0, 0), (0, pad), (0, 0)))
    kblk = kh.reshape(B, H, nb, MOBA_BLOCK, D)
    vblk = vh.reshape(B, H, nb, MOBA_BLOCK, D)
    kmean = jnp.mean(kblk.astype(F32), axis=3)
    qc_size = min(MOBA_Q_CHUNK, S)
    n_chunks = S // qc_size
    bidx = jnp.arange(B)[:, None, None, None]
    hidx = jnp.arange(H)[None, :, None, None]

    def fetch(idx):
        return kblk[bidx, hidx, idx], vblk[bidx, hidx, idx]

    def one_chunk(ci):
        start = ci * qc_size
        qc = lax.dynamic_slice_in_dim(qh, start, qc_size, axis=2)
        qpos = start + jnp.arange(qc_size, dtype=jnp.int32)
        qb = start // MOBA_BLOCK
        own_k = lax.dynamic_index_in_dim(kblk, qb, axis=2, keepdims=False)
        own_v = lax.dynamic_index_in_dim(vblk, qb, axis=2, keepdims=False)
        own_pos = qb * MOBA_BLOCK + jnp.arange(MOBA_BLOCK, dtype=jnp.int32)
        return moba_core(qc, qpos, qb, kmean, fetch, own_k, own_v, own_pos)

    out = lax.map(one_chunk, jnp.arange(n_chunks, dtype=jnp.int32))
    return jnp.transpose(out, (1, 0, 3, 2, 4)).reshape(B, S, H, D)


def moba_sample(cache_k, cache_v, page_table, layer, q, k, v):
    B, L, H, D = q.shape
    n_pages = page_table.shape[1]
    past = n_pages * PAGE_SIZE
    ppb = MOBA_BLOCK // PAGE_SIZE
    nb = -(-n_pages // ppb)
    k_past = cache_k[layer, page_table]
    page_sum = jnp.sum(k_past.astype(F32), axis=2)
    page_sum = jnp.pad(page_sum, ((0, 0), (0, nb * ppb - n_pages), (0, 0), (0, 0)))
    kmean = jnp.transpose(page_sum.reshape(B, nb, ppb, H, D).sum(axis=2) / MOBA_BLOCK, (0, 2, 1, 3))
    qb = past // MOBA_BLOCK
    qpos = past + jnp.arange(L, dtype=jnp.int32)
    bidx = jnp.arange(B)[:, None, None, None, None]
    hidx = jnp.arange(H)[None, :, None, None, None]

    def fetch(idx):
        pos = jnp.minimum(idx[..., None] * MOBA_BLOCK + jnp.arange(MOBA_BLOCK, dtype=jnp.int32), past - 1)
        phys = page_table[bidx, pos // PAGE_SIZE]
        off = pos % PAGE_SIZE
        return cache_k[layer, phys, off, hidx], cache_v[layer, phys, off, hidx]

    own_start = qb * MOBA_BLOCK
    own_pos_past = own_start + jnp.arange(past - own_start, dtype=jnp.int32)
    phys_o = page_table[:, own_pos_past // PAGE_SIZE]
    off_o = own_pos_past % PAGE_SIZE
    own_k = jnp.concatenate([cache_k[layer, phys_o, off_o].astype(k.dtype), k], axis=1)
    own_v = jnp.concatenate([cache_v[layer, phys_o, off_o].astype(v.dtype), v], axis=1)
    own_pos = jnp.concatenate([own_pos_past, qpos])
    out = moba_core(jnp.transpose(q, (0, 2, 1, 3)), qpos, qb, kmean, fetch,
                    jnp.transpose(own_k, (0, 2, 1, 3)), jnp.transpose(own_v, (0, 2, 1, 3)), own_pos)
    return jnp.transpose(out, (0, 2, 1, 3))


def decoder_layer(x, positions, conv_buf, dn_state, moba_fn, w_pre_mix, w_in, conv_w, a_log, dt_bias,
                  dn_norm_w, w_out, w_post_mix, w_pre_mlp, w_up, w_down, w_post_mlp):
    B, L, _ = x.shape
    h = rmsnorm(x, w_pre_mix)
    proj = h @ w_in
    cuts = [MOBA_WIDTH, 2 * MOBA_WIDTH, 3 * MOBA_WIDTH, 3 * MOBA_WIDTH + 3 * DN_WIDTH,
            3 * MOBA_WIDTH + 4 * DN_WIDTH, 3 * MOBA_WIDTH + 4 * DN_WIDTH + N_DN_HEADS]
    q_a, k_a, v_a, qkv_d, z_d, b_d, a_d = jnp.split(proj, cuts, axis=-1)
    q_a = rope(q_a.reshape(B, L, N_MOBA_HEADS, HEAD_DIM), positions)
    k_a = rope(k_a.reshape(B, L, N_MOBA_HEADS, HEAD_DIM), positions)
    v_a = v_a.reshape(B, L, N_MOBA_HEADS, HEAD_DIM)
    o_a = moba_fn(q_a, k_a, v_a)
    o_d, new_buf, new_state = gated_deltanet(qkv_d, z_d, b_d, a_d, conv_buf, dn_state, conv_w,
                                             a_log, dt_bias, dn_norm_w)
    mix = jnp.concatenate([o_a.reshape(B, L, MOBA_WIDTH), o_d], axis=-1) @ w_out
    x = x + rmsnorm(mix, w_post_mix)
    h = rmsnorm(x, w_pre_mlp)
    f = jnp.square(jax.nn.relu(h @ w_up)) @ w_down
    x = x + rmsnorm(f, w_post_mlp)
    return x, k_a, v_a, new_buf, new_state


def setup_inputs(seed: int = 0) -> dict:
    key = jax.random.key(seed)
    ks = jax.random.split(key, 20)
    n_pages = PAST_LEN // PAGE_SIZE
    n_used = DEC_BATCH * n_pages
    n_pool = n_used + max(1, n_used // 4)

    def normal(k, shape, scale):
        return scale * jax.random.normal(k, shape, F32)

    x_prompt = normal(ks[0], (BATCH, SEQ, D_MODEL), 1.0)
    x_sample = normal(ks[1], (DEC_BATCH, DEC_SEQ, D_MODEL), 1.0)
    cache_k = normal(ks[2], (DEPTH, n_pool, PAGE_SIZE, N_MOBA_HEADS, HEAD_DIM), 1.0)
    cache_v = normal(ks[3], (DEPTH, n_pool, PAGE_SIZE, N_MOBA_HEADS, HEAD_DIM), 1.0)
    state_dn = normal(ks[4], (DEPTH, DEC_BATCH, N_DN_HEADS, HEAD_DIM, HEAD_DIM), 0.1)
    state_conv = normal(ks[5], (DEPTH, DEC_BATCH, CONV_W - 1, 3 * DN_WIDTH), 1.0)
    page_table = jax.random.permutation(ks[6], n_pool)[:n_used].reshape(DEC_BATCH, n_pages).astype(jnp.int32)
    w_pre_mix = 1.0 + normal(ks[7], (DEPTH, D_MODEL), 0.02)
    w_in = normal(ks[8], (DEPTH, D_MODEL, PROJ_DIM), D_MODEL ** -0.5)
    conv_w = normal(ks[9], (DEPTH, CONV_W, 3 * DN_WIDTH), CONV_W ** -0.5)
    a_log = jnp.log(jax.random.uniform(ks[10], (DEPTH, N_DN_HEADS), F32, 1.0, 16.0))
    dt_bias = normal(ks[11], (DEPTH, N_DN_HEADS), 0.1)
    dn_norm_w = 1.0 + normal(ks[12], (DEPTH, HEAD_DIM), 0.02)
    w_out = normal(ks[13], (DEPTH, MOBA_WIDTH + DN_WIDTH, D_MODEL), (MOBA_WIDTH + DN_WIDTH) ** -0.5)
    w_post_mix = 1.0 + normal(ks[14], (DEPTH, D_MODEL), 0.02)
    w_pre_mlp = 1.0 + normal(ks[15], (DEPTH, D_MODEL), 0.02)
    w_up = normal(ks[16], (DEPTH, D_MODEL, D_FF), D_MODEL ** -0.5)
    w_down = normal(ks[17], (DEPTH, D_FF, D_MODEL), D_FF ** -0.5)
    w_post_mlp = 1.0 + normal(ks[18], (DEPTH, D_MODEL), 0.02)
    return {"x_prompt": x_prompt, "x_sample": x_sample, "cache_k": cache_k, "cache_v": cache_v,
            "state_dn": state_dn, "state_conv": state_conv, "page_table": page_table,
            "w_pre_mix": w_pre_mix, "w_in": w_in, "conv_w": conv_w, "a_log": a_log, "dt_bias": dt_bias,
            "dn_norm_w": dn_norm_w, "w_out": w_out, "w_post_mix": w_post_mix, "w_pre_mlp": w_pre_mlp,
            "w_up": w_up, "w_down": w_down, "w_post_mlp": w_post_mlp}


def reference(x_prompt, x_sample, cache_k, cache_v, state_dn, state_conv, page_table, w_pre_mix, w_in,
              conv_w, a_log, dt_bias, dn_norm_w, w_out, w_post_mix, w_pre_mlp, w_up, w_down, w_post_mlp):
    B, S, _ = x_prompt.shape
    DB, L, _ = x_sample.shape
    past = page_table.shape[1] * PAGE_SIZE
    pos_p = jnp.arange(S, dtype=jnp.int32)
    pos_s = past + jnp.arange(L, dtype=jnp.int32)
    hp, hs = x_prompt, x_sample
    kp_l, vp_l, ks_l, vs_l, dnp_l, dns_l, cvp_l, cvs_l = [], [], [], [], [], [], [], []
    for l in range(DEPTH):
        lw = (w_pre_mix[l], w_in[l], conv_w[l], a_log[l], dt_bias[l], dn_norm_w[l], w_out[l],
              w_post_mix[l], w_pre_mlp[l], w_up[l], w_down[l], w_post_mlp[l])
        zero_buf = jnp.zeros((B, CONV_W - 1, 3 * DN_WIDTH), x_prompt.dtype)
        zero_state = jnp.zeros((B, N_DN_HEADS, HEAD_DIM, HEAD_DIM), state_dn.dtype)
        hp, kp, vp, cvp, dnp = decoder_layer(hp, pos_p, zero_buf, zero_state, moba_prompt, *lw)
        sample_moba = functools.partial(moba_sample, cache_k, cache_v, page_table, l)
        hs, kss, vss, cvs, dns = decoder_layer(hs, pos_s, state_conv[l], state_dn[l], sample_moba, *lw)
        kp_l.append(kp); vp_l.append(vp); ks_l.append(kss); vs_l.append(vss)
        dnp_l.append(dnp); dns_l.append(dns); cvp_l.append(cvp); cvs_l.append(cvs)
    k_prompt = jnp.stack(kp_l)
    v_prompt = jnp.stack(vp_l)
    k_sample = jnp.stack(ks_l)
    v_sample = jnp.stack(vs_l)
    dn_prompt = jnp.stack(dnp_l)
    dn_sample = jnp.stack(dns_l)
    conv_prompt = jnp.stack(cvp_l)
    conv_sample = jnp.stack(cvs_l)
    return (hp, hs, k_prompt, v_prompt, k_sample, v_sample, dn_prompt, dn_sample, conv_prompt, conv_sample)
```

```python
import functools
import math

import jax
import jax.numpy as jnp
from jax import lax
from jax.experimental import pallas as pl
from jax.experimental.pallas import tpu as pltpu

F32 = jnp.float32
BF16 = jnp.bfloat16
HIGHEST = lax.Precision.HIGHEST

HEAD_DIM = 128
N_HEADS = 8
WIDTH = N_HEADS * HEAD_DIM
MOBA_BLOCK = 256
MOBA_TOPK = 3
DN_CHUNK = 64
CONV_W = 4
ROPE_THETA = 10000.0
NORM_EPS = 1e-6
ATTN_SCALE = HEAD_DIM ** -0.5
LANES = 128
SUBLANES = 8
VMEM_LIMIT = 56 * 1024 * 1024

_NT = (((1,), (1,)), ((), ()))
_TN = (((0,), (0,)), ((), ()))


def _params(*sem):
    return pltpu.CompilerParams(dimension_semantics=sem, vmem_limit_bytes=VMEM_LIMIT)


def _rms(x, w):
    return x * lax.rsqrt(jnp.mean(x * x, axis=-1, keepdims=True) + NORM_EPS) * w


def _silu(x):
    return x * jax.nn.sigmoid(x)


def _in_proj_body(x_ref, wn_ref, w_ref, wba_ref, cos_ref, sin_ref,
                  q_ref, k_ref, v_ref, d_ref, z_ref, ba_ref, h_scr, *, tn, bounds):
    j = pl.program_id(1)

    @pl.when(j == 0)
    def _():
        h = _rms(x_ref[...], wn_ref[...]).astype(BF16)
        h_scr[...] = h
        ba_ref[...] = jnp.dot(h, wba_ref[...], preferred_element_type=F32)

    acc = jnp.dot(h_scr[...], w_ref[...], preferred_element_type=F32)

    def rope(a):
        cos, sin = cos_ref[...], sin_ref[...]
        heads = [a[:, c:c + HEAD_DIM] * cos + pltpu.roll(a[:, c:c + HEAD_DIM], HEAD_DIM // 2, 1) * sin
                 for c in range(0, tn, HEAD_DIM)]
        return jnp.concatenate(heads, axis=1)

    bq, bk, bv, bd = bounds

    @pl.when(j < bq)
    def _():
        q_ref[...] = rope(acc).astype(q_ref.dtype)

    @pl.when((j >= bq) & (j < bk))
    def _():
        k_ref[...] = rope(acc)

    @pl.when((j >= bk) & (j < bv))
    def _():
        v_ref[...] = acc

    @pl.when((j >= bv) & (j < bd))
    def _():
        d_ref[...] = acc

    @pl.when(j >= bd)
    def _():
        z_ref[...] = acc


def _in_proj(x, w_norm, w_main, w_ba, cos, sin, *, tm, q_dtype):
    t, d = x.shape
    tn = 512
    widths = (WIDTH, WIDTH, WIDTH, 3 * WIDTH, WIDTH)
    nblk = [w // tn for w in widths]
    starts = [sum(nblk[:g]) for g in range(len(nblk))]
    bounds = tuple(starts[g] + nblk[g] for g in range(4))
    n_rope = cos.shape[0] // tm

    def out_spec(g):
        return pl.BlockSpec((tm, tn), lambda i, j, g=g: (i, jnp.clip(j - starts[g], 0, nblk[g] - 1)))

    return pl.pallas_call(
        functools.partial(_in_proj_body, tn=tn, bounds=bounds),
        grid=(t // tm, sum(nblk)),
        in_specs=[
            pl.BlockSpec((tm, d), lambda i, j: (i, 0)),
            pl.BlockSpec((1, d), lambda i, j: (0, 0)),
            pl.BlockSpec((d, tn), lambda i, j: (0, j)),
            pl.BlockSpec((d, LANES), lambda i, j: (0, 0)),
            pl.BlockSpec((tm, HEAD_DIM), lambda i, j: (i % n_rope, 0)),
            pl.BlockSpec((tm, HEAD_DIM), lambda i, j: (i % n_rope, 0)),
        ],
        out_specs=[out_spec(0), out_spec(1), out_spec(2), out_spec(3), out_spec(4),
                   pl.BlockSpec((tm, LANES), lambda i, j: (i, 0))],
        out_shape=[
            jax.ShapeDtypeStruct((t, WIDTH), q_dtype),
            jax.ShapeDtypeStruct((t, WIDTH), F32),
            jax.ShapeDtypeStruct((t, WIDTH), F32),
            jax.ShapeDtypeStruct((t, 3 * WIDTH), F32),
            jax.ShapeDtypeStruct((t, WIDTH), F32),
            jax.ShapeDtypeStruct((t, LANES), F32),
        ],
        scratch_shapes=[pltpu.VMEM((tm, d), BF16)],
        compiler_params=_params("arbitrary", "arbitrary"),
        name="in_proj",
    )(x, w_norm, w_main, w_ba, cos, sin)


def _topk_mask(gate, valid, axis):
    n = gate.shape[axis]
    idx = lax.broadcasted_iota(jnp.int32, gate.shape, axis)
    g = jnp.where(valid, gate, -jnp.inf)
    sel = jnp.zeros(gate.shape, jnp.bool_)
    for _ in range(min(MOBA_TOPK, n)):
        m = jnp.max(g, axis=axis, keepdims=True)
        first = jnp.min(jnp.where(g == m, idx, n), axis=axis, keepdims=True)
        hit = idx == first
        sel = sel | hit
        g = jnp.where(hit, -jnp.inf, g)
    return sel & valid


def _moba_prompt_body(q_ref, k_ref, v_ref, o_ref, kb_scr, vt_scr, km_scr, sel_scr, *, nb):
    blk = MOBA_BLOCK
    qi = pl.program_id(2)

    @pl.when(qi == 0)
    def _():
        for j in range(nb):
            rows = slice(j * blk, (j + 1) * blk)
            k = k_ref[rows, :]
            kb_scr[rows, :] = k.astype(BF16)
            vt_scr[:, rows] = v_ref[rows, :].T.astype(BF16)
            km_scr[j:j + 1, :] = jnp.sum(k, axis=0, keepdims=True) / blk

    q = q_ref[...]
    gate = lax.dot_general(km_scr[...], q.astype(F32), _NT, precision=HIGHEST,
                           preferred_element_type=F32)
    block_id = lax.broadcasted_iota(jnp.int32, gate.shape, 0)
    sel_scr[...] = _topk_mask(gate, block_id < qi, 0).astype(F32)

    own = pl.multiple_of(qi * blk, blk)
    s = lax.dot_general(kb_scr[pl.ds(own, blk), :], q, _NT, preferred_element_type=F32) * ATTN_SCALE
    kpos = lax.broadcasted_iota(jnp.int32, s.shape, 0)
    qpos = lax.broadcasted_iota(jnp.int32, s.shape, 1)
    s = jnp.where(kpos <= qpos, s, -jnp.inf)
    m = jnp.max(s, axis=0, keepdims=True)
    p = jnp.exp(s - m)
    l = jnp.sum(p, axis=0, keepdims=True)
    acc = jnp.dot(vt_scr[:, pl.ds(own, blk)], p.astype(BF16), preferred_element_type=F32)

    def past_block(j, carry):
        m, l, acc = carry
        off = pl.multiple_of(j * blk, blk)
        s = lax.dot_general(kb_scr[pl.ds(off, blk), :], q, _NT, preferred_element_type=F32) * ATTN_SCALE
        s = jnp.where(sel_scr[pl.ds(j, 1), :] > 0, s, -jnp.inf)
        m_new = jnp.maximum(m, jnp.max(s, axis=0, keepdims=True))
        alpha = jnp.exp(m - m_new)
        p = jnp.exp(s - m_new)
        l = alpha * l + jnp.sum(p, axis=0, keepdims=True)
        acc = alpha * acc + jnp.dot(vt_scr[:, pl.ds(off, blk)], p.astype(BF16), preferred_element_type=F32)
        return m_new, l, acc

    m, l, acc = lax.fori_loop(0, qi, past_block, (m, l, acc))
    o_ref[...] = (acc / l).T.astype(o_ref.dtype)


def _moba_prompt(q, k, v, batch, seq):
    assert seq % MOBA_BLOCK == 0
    nb = seq // MOBA_BLOCK
    t = batch * seq
    return pl.pallas_call(
        functools.partial(_moba_prompt_body, nb=nb),
        grid=(batch, N_HEADS, nb),
        in_specs=[
            pl.BlockSpec((MOBA_BLOCK, HEAD_DIM), lambda b, h, i: (b * nb + i, h)),
            pl.BlockSpec((seq, HEAD_DIM), lambda b, h, i: (b, h)),
            pl.BlockSpec((seq, HEAD_DIM), lambda b, h, i: (b, h)),
        ],
        out_specs=pl.BlockSpec((MOBA_BLOCK, HEAD_DIM), lambda b, h, i: (b * nb + i, h)),
        out_shape=jax.ShapeDtypeStruct((t, WIDTH), BF16),
        scratch_shapes=[
            pltpu.VMEM((seq, HEAD_DIM), BF16),
            pltpu.VMEM((HEAD_DIM, seq), BF16),
            pltpu.VMEM((nb, HEAD_DIM), F32),
            pltpu.VMEM((nb, MOBA_BLOCK), F32),
        ],
        compiler_params=_params("arbitrary", "arbitrary", "arbitrary"),
        name="moba_prompt",
    )(q, k, v)


def _page_sum_body(pt_ref, *refs, n_in, ppb):
    del pt_ref
    out_ref = refs[n_in]
    for i in range(n_in // ppb):
        tot = jnp.sum(refs[i * ppb][...], axis=0)
        for r in range(1, ppb):
            tot = tot + jnp.sum(refs[i * ppb + r][...], axis=0)
        out_ref[i] = tot / MOBA_BLOCK


def _block_means(cache_k, page_table_flat, layer, dec_batch, n_pages):
    page = cache_k.shape[2]
    ppb = MOBA_BLOCK // page
    assert n_pages % ppb == 0
    per_step = ppb * SUBLANES
    while n_pages % per_step:
        per_step //= 2
    assert per_step % ppb == 0
    steps = n_pages // per_step
    nb = n_pages // ppb

    def spec(r):
        return pl.BlockSpec((None, None, page, N_HEADS, HEAD_DIM),
                            lambda b, p, pt, r=r: (layer, pt[b * n_pages + p * per_step + r], 0, 0, 0))

    return pl.pallas_call(
        functools.partial(_page_sum_body, n_in=per_step, ppb=ppb),
        grid_spec=pltpu.PrefetchScalarGridSpec(
            num_scalar_prefetch=1,
            grid=(dec_batch, steps),
            in_specs=[spec(r) for r in range(per_step)],
            out_specs=pl.BlockSpec((None, per_step // ppb, N_HEADS, HEAD_DIM), lambda b, p, pt: (b, p, 0, 0)),
        ),
        out_shape=jax.ShapeDtypeStruct((dec_batch, nb, N_HEADS, HEAD_DIM), F32),
        compiler_params=_params("arbitrary", "arbitrary"),
        name="moba_block_means",
    )(page_table_flat, *([cache_k] * per_step))


def _sample_select_body(q_ref, km_ref, idx_ref, *, dec_seq):
    for h in range(N_HEADS):
        q = q_ref[:, h * HEAD_DIM:(h + 1) * HEAD_DIM]
        gate = lax.dot_general(q, km_ref[h], _NT, precision=HIGHEST, preferred_element_type=F32)
        nb = gate.shape[1]
        idx = lax.broadcasted_iota(jnp.int32, gate.shape, 1)
        lane = lax.broadcasted_iota(jnp.int32, (dec_seq, LANES), 1)
        out = jnp.zeros((dec_seq, LANES), jnp.int32)
        g = gate
        for r in range(MOBA_TOPK):
            m = jnp.max(g, axis=1, keepdims=True)
            first = jnp.min(jnp.where(g == m, idx, nb), axis=1, keepdims=True)
            out = jnp.where(lane == r, first, out)
            g = jnp.where(idx == first, -jnp.inf, g)
        idx_ref[h] = out


def _sample_select(q, kmean_t, dec_batch, dec_seq):
    nb = kmean_t.shape[2]
    return pl.pallas_call(
        functools.partial(_sample_select_body, dec_seq=dec_seq),
        grid=(dec_batch,),
        in_specs=[
            pl.BlockSpec((dec_seq, WIDTH), lambda b: (b, 0)),
            pl.BlockSpec((None, N_HEADS, nb, HEAD_DIM), lambda b: (b, 0, 0, 0)),
        ],
        out_specs=pl.BlockSpec((None, N_HEADS, dec_seq, LANES), lambda b: (b, 0, 0, 0)),
        out_shape=jax.ShapeDtypeStruct((dec_batch, N_HEADS, dec_seq, LANES), jnp.int32),
        compiler_params=_params("arbitrary"),
        name="moba_sample_select",
    )(q, kmean_t)


def _sample_attn_body(pages_ref, q_ref, kn_ref, vn_ref, ck_hbm, cv_hbm, o_ref, kbuf, vbuf, sem,
                      *, layer, dec_seq, ppb, page):
    b = pl.program_id(0)
    h = pl.program_id(1)
    n_sel = MOBA_TOPK * ppb
    base = (b * N_HEADS + h) * dec_seq * n_sel

    def slab_copies(qi, s):
        phys = pages_ref[base + qi * n_sel + s]
        dst = pl.ds(s * page, page)
        return (pltpu.make_async_copy(ck_hbm.at[layer, phys, :, h, :], kbuf.at[qi, dst, :], sem.at[0]),
                pltpu.make_async_copy(cv_hbm.at[layer, phys, :, h, :], vbuf.at[qi, dst, :], sem.at[1]))

    for qi in range(dec_seq):
        for s in range(n_sel):
            for c in slab_copies(qi, s):
                c.start()
    for qi in range(dec_seq):
        for s in range(n_sel):
            for c in slab_copies(qi, s):
                c.wait()

    q = q_ref[...]
    qb = q.astype(BF16)
    kn = kn_ref[...]
    vn = vn_ref[...]
    s_own = lax.dot_general(kn.astype(BF16), qb, _NT, preferred_element_type=F32) * ATTN_SCALE
    kpos = lax.broadcasted_iota(jnp.int32, s_own.shape, 0)
    qpos = lax.broadcasted_iota(jnp.int32, s_own.shape, 1)
    s_own = jnp.where(kpos <= qpos, s_own, -jnp.inf)
    rows = []
    for qi in range(dec_seq):
        s_all = lax.dot_general(kbuf[qi].astype(BF16), qb, _NT, preferred_element_type=F32) * ATTN_SCALE
        s_sel = s_all[:, qi:qi + 1]
        s_new = s_own[:, qi:qi + 1]
        m = jnp.maximum(jnp.max(s_sel, axis=0, keepdims=True), jnp.max(s_new, axis=0, keepdims=True))
        p_sel = jnp.exp(s_sel - m)
        p_new = jnp.exp(s_new - m)
        l = jnp.sum(p_sel, axis=0, keepdims=True) + jnp.sum(p_new, axis=0, keepdims=True)
        pv = jnp.sum(p_sel * vbuf[qi], axis=0, keepdims=True) + jnp.sum(p_new * vn, axis=0, keepdims=True)
        rows.append(pv / l)
    o_ref[...] = jnp.concatenate(rows, axis=0).astype(o_ref.dtype)


def _sample_attn(pages_flat, q, k_new, v_new, cache_k, cache_v, layer, dec_batch, dec_seq):
    page = cache_k.shape[2]
    ppb = MOBA_BLOCK // page
    n_keys = MOBA_TOPK * MOBA_BLOCK
    row = lambda b, h, pg: (b, h)
    return pl.pallas_call(
        functools.partial(_sample_attn_body, layer=layer, dec_seq=dec_seq, ppb=ppb, page=page),
        grid_spec=pltpu.PrefetchScalarGridSpec(
            num_scalar_prefetch=1,
            grid=(dec_batch, N_HEADS),
            in_specs=[
                pl.BlockSpec((dec_seq, HEAD_DIM), row),
                pl.BlockSpec((dec_seq, HEAD_DIM), row),
                pl.BlockSpec((dec_seq, HEAD_DIM), row),
                pl.BlockSpec(memory_space=pl.ANY),
                pl.BlockSpec(memory_space=pl.ANY),
            ],
            out_specs=pl.BlockSpec((dec_seq, HEAD_DIM), row),
            scratch_shapes=[
                pltpu.VMEM((dec_seq, n_keys, HEAD_DIM), F32),
                pltpu.VMEM((dec_seq, n_keys, HEAD_DIM), F32),
                pltpu.SemaphoreType.DMA((2,)),
            ],
        ),
        out_shape=jax.ShapeDtypeStruct((dec_batch * dec_seq, WIDTH), F32),
        compiler_params=_params("arbitrary", "arbitrary"),
        name="moba_sample_attn",
    )(pages_flat, q, k_new, v_new, cache_k, cache_v)


def _moba_sample(q, k_new, v_new, cache_k, cache_v, page_table, layer, dec_batch, dec_seq):
    n_pages = page_table.shape[1]
    page = cache_k.shape[2]
    ppb = MOBA_BLOCK // page
    past = n_pages * page
    assert past % MOBA_BLOCK == 0 and dec_seq <= MOBA_BLOCK and past // MOBA_BLOCK >= MOBA_TOPK
    assert dec_seq % SUBLANES == 0
    pt_flat = page_table.reshape(-1)
    kmean = _block_means(cache_k, pt_flat, layer, dec_batch, n_pages)
    idx = _sample_select(q, jnp.transpose(kmean, (0, 2, 1, 3)), dec_batch, dec_seq)[..., :MOBA_TOPK]
    logical = idx[..., None] * ppb + jnp.arange(ppb, dtype=jnp.int32)
    pages = jnp.take_along_axis(page_table, logical.reshape(dec_batch, -1), axis=1)
    o = _sample_attn(pages.reshape(-1), q, k_new, v_new, cache_k, cache_v, layer, dec_batch, dec_seq)
    return o.astype(BF16)


def _deltanet_body(x_ref, z_ref, ba_ref, cbuf_ref, cw_ref, alog_ref, dtb_ref, nw_ref, s0_ref,
                   o_ref, sout_ref, xp_scr, s_scr, *, n_valid):
    c = DN_CHUNK
    step = pl.program_id(1)

    @pl.when(step == 0)
    def _():
        s_scr[...] = s0_ref[...]
        xp_scr[0:SUBLANES, :] = cbuf_ref[...]

    @pl.when(step > 0)
    def _():
        xp_scr[0:SUBLANES, :] = xp_scr[c:c + SUBLANES, :]

    xp_scr[SUBLANES:, :] = x_ref[...]

    def conv(col):
        lanes = slice(col, col + HEAD_DIM)
        first = SUBLANES - (CONV_W - 1)
        y = xp_scr[first:first + c, lanes] * cw_ref[0:1, lanes]
        for j in range(1, CONV_W):
            y = y + xp_scr[first + j:first + j + c, lanes] * cw_ref[j:j + 1, lanes]
        return _silu(y)

    def l2n(a):
        return a * lax.rsqrt(jnp.sum(a * a, axis=-1, keepdims=True) + NORM_EPS)

    ba = ba_ref[...]
    beta = jax.nn.sigmoid(ba)
    xs = ba + dtb_ref[...]
    g = -jnp.exp(alog_ref[...]) * (jnp.maximum(xs, 0.0) + jnp.log1p(jnp.exp(-jnp.abs(xs))))
    if n_valid < c:
        live = lax.broadcasted_iota(jnp.int32, ba.shape, 0) < n_valid
        beta = jnp.where(live, beta, 0.0)
        g = jnp.where(live, g, 0.0)

    ri = lax.broadcasted_iota(jnp.int32, (c, c), 0)
    ci = lax.broadcasted_iota(jnp.int32, (c, c), 1)
    causal = ri >= ci
    strict = ri > ci
    eye = (ri == ci).astype(F32)
    gc = jnp.dot(causal.astype(F32), g, precision=HIGHEST, preferred_element_type=F32)
    gct = gc.T
    eg = jnp.exp(gc)
    g_last = gc[c - 1:c, :]
    kdf = jnp.exp(g_last - gc)
    g_tot = jnp.exp(g_last)

    def mm(a, b):
        return jnp.dot(a, b, precision=HIGHEST, preferred_element_type=F32)

    for h in range(N_HEADS):
        lane_g = N_HEADS + h
        qn = l2n(conv(h * HEAD_DIM)) * (HEAD_DIM ** -0.5)
        kn = l2n(conv(WIDTH + h * HEAD_DIM))
        v = conv(2 * WIDTH + h * HEAD_DIM)
        diff = gc[:, lane_g:lane_g + 1] - gct[lane_g:lane_g + 1, :]
        decay = jnp.where(causal, jnp.exp(jnp.where(causal, diff, 0.0)), 0.0)
        b_col = beta[:, h:h + 1]
        e_col = eg[:, lane_g:lane_g + 1]
        kb = kn * b_col
        vb = v * b_col
        a_kk = jnp.where(strict, lax.dot_general(kb, kn, _NT, precision=HIGHEST, preferred_element_type=F32) * decay, 0.0)
        pw = -a_kk
        t_inv = eye + pw
        for _ in range(int(math.log2(c)) - 1):
            pw = mm(pw, pw)
            t_inv = t_inv + mm(t_inv, pw)
        u = mm(t_inv, vb)
        w = mm(t_inv, kb * e_col)
        a_qk = jnp.where(causal, lax.dot_general(qn, kn, _NT, precision=HIGHEST, preferred_element_type=F32) * decay, 0.0)
        s = s_scr[h]
        v_new = u - mm(w, s)
        o = mm(qn * e_col, s) + mm(a_qk, v_new)
        k_dec = kn * kdf[:, lane_g:lane_g + 1]
        s_scr[h] = s * g_tot[:, lane_g:lane_g + 1] + lax.dot_general(k_dec, v_new, _TN, precision=HIGHEST,
                                                                      preferred_element_type=F32)
        lanes = slice(h * HEAD_DIM, (h + 1) * HEAD_DIM)
        o_ref[:, lanes] = (_rms(o, nw_ref[...]) * _silu(z_ref[:, lanes])).astype(o_ref.dtype)

    @pl.when(step == pl.num_programs(1) - 1)
    def _():
        sout_ref[...] = s_scr[...]


def _deltanet(qkv, z, ba, conv_buf, state0, conv_w, a_log, dt_bias, norm_w, batch, seq, n_valid):
    c = DN_CHUNK
    assert seq % c == 0 and (n_valid == c or seq == c)
    n_chunks = seq // c
    cw3 = 3 * WIDTH
    cbuf = jnp.pad(conv_buf, ((0, 0), (SUBLANES - (CONV_W - 1), 0), (0, 0)))
    pad = (0, LANES - 2 * N_HEADS)
    alog = jnp.pad(jnp.concatenate([jnp.zeros_like(a_log), a_log]), pad).reshape(1, LANES)
    dtb = jnp.pad(jnp.concatenate([jnp.zeros_like(dt_bias), dt_bias]), pad).reshape(1, LANES)
    row = lambda b, i: (b * n_chunks + i, 0)
    fixed = lambda b, i: (0, 0)
    return pl.pallas_call(
        functools.partial(_deltanet_body, n_valid=n_valid),
        grid=(batch, n_chunks),
        in_specs=[
            pl.BlockSpec((c, cw3), row),
            pl.BlockSpec((c, WIDTH), row),
            pl.BlockSpec((c, LANES), row),
            pl.BlockSpec((None, SUBLANES, cw3), lambda b, i: (b, 0, 0)),
            pl.BlockSpec((CONV_W, cw3), fixed),
            pl.BlockSpec((1, LANES), fixed),
            pl.BlockSpec((1, LANES), fixed),
            pl.BlockSpec((1, HEAD_DIM), fixed),
            pl.BlockSpec((None, N_HEADS, HEAD_DIM, HEAD_DIM), lambda b, i: (b, 0, 0, 0)),
        ],
        out_specs=[
            pl.BlockSpec((c, WIDTH), row),
            pl.BlockSpec((None, N_HEADS, HEAD_DIM, HEAD_DIM), lambda b, i: (b, 0, 0, 0)),
        ],
        out_shape=[
            jax.ShapeDtypeStruct((batch * seq, WIDTH), BF16),
            jax.ShapeDtypeStruct((batch, N_HEADS, HEAD_DIM, HEAD_DIM), F32),
        ],
        scratch_shapes=[
            pltpu.VMEM((c + SUBLANES, cw3), F32),
            pltpu.VMEM((N_HEADS, HEAD_DIM, HEAD_DIM), F32),
        ],
        compiler_params=_params("arbitrary", "arbitrary"),
        name="deltanet",
    )(qkv, z, ba, cbuf, conv_w, alog, dtb, norm_w.reshape(1, HEAD_DIM), state0)


def _out_proj_body(oa_ref, od_ref, x_ref, w_ref, wpost_ref, wpre_ref, x1_ref, h2_ref):
    mix = (jnp.dot(oa_ref[...], w_ref[0:WIDTH, :], preferred_element_type=F32)
           + jnp.dot(od_ref[...], w_ref[WIDTH:, :], preferred_element_type=F32))
    x1 = x_ref[...] + _rms(mix, wpost_ref[...])
    x1_ref[...] = x1
    h2_ref[...] = _rms(x1, wpre_ref[...]).astype(h2_ref.dtype)


def _out_proj(o_a, o_d, x, w_out, w_post, w_pre, *, tm):
    t, d = x.shape
    row = lambda i: (i, 0)
    fixed = lambda i: (0, 0)
    return pl.pallas_call(
        _out_proj_body,
        grid=(t // tm,),
        in_specs=[
            pl.BlockSpec((tm, WIDTH), row),
            pl.BlockSpec((tm, WIDTH), row),
            pl.BlockSpec((tm, d), row),
            pl.BlockSpec((2 * WIDTH, d), fixed),
            pl.BlockSpec((1, d), fixed),
            pl.BlockSpec((1, d), fixed),
        ],
        out_specs=[pl.BlockSpec((tm, d), row), pl.BlockSpec((tm, d), row)],
        out_shape=[jax.ShapeDtypeStruct((t, d), F32), jax.ShapeDtypeStruct((t, d), BF16)],
        compiler_params=_params("arbitrary"),
        name="out_proj",
    )(o_a, o_d, x, w_out, w_post, w_pre)


def _mlp_body(h_ref, x1_ref, wu_ref, wd_ref, wpost_ref, y_ref, acc_scr):
    f = pl.program_id(1)
    a = jnp.dot(h_ref[...], wu_ref[...], preferred_element_type=F32)
    a = jnp.square(jnp.maximum(a, 0.0)).astype(BF16)
    part = jnp.dot(a, wd_ref[...], preferred_element_type=F32)

    @pl.when(f == 0)
    def _():
        acc_scr[...] = part

    @pl.when(f > 0)
    def _():
        acc_scr[...] += part

    @pl.when(f == pl.num_programs(1) - 1)
    def _():
        y_ref[...] = x1_ref[...] + _rms(acc_scr[...], wpost_ref[...])


def _mlp(h2, x1, w_up, w_down, w_post, *, tm, tf):
    t, d = x1.shape
    d_ff = w_up.shape[1]
    return pl.pallas_call(
        _mlp_body,
        grid=(t // tm, d_ff // tf),
        in_specs=[
            pl.BlockSpec((tm, d), lambda i, f: (i, 0)),
            pl.BlockSpec((tm, d), lambda i, f: (i, 0)),
            pl.BlockSpec((d, tf), lambda i, f: (0, f)),
            pl.BlockSpec((tf, d), lambda i, f: (f, 0)),
            pl.BlockSpec((1, d), lambda i, f: (0, 0)),
        ],
        out_specs=pl.BlockSpec((tm, d), lambda i, f: (i, 0)),
        out_shape=jax.ShapeDtypeStruct((t, d), F32),
        scratch_shapes=[pltpu.VMEM((tm, d), F32)],
        compiler_params=_params("arbitrary", "arbitrary"),
        name="mlp",
    )(h2, x1, w_up, w_down, w_post)


def _rope_tables(pos, rows):
    half = HEAD_DIM // 2
    inv_freq = 1.0 / (ROPE_THETA ** (jnp.arange(half, dtype=F32) / half))
    ang = pos.astype(F32)[:, None] * inv_freq[None, :]
    cos = jnp.concatenate([jnp.cos(ang), jnp.cos(ang)], axis=1)
    sin = jnp.concatenate([-jnp.sin(ang), jnp.sin(ang)], axis=1)
    reps = max(1, rows // pos.shape[0])
    return jnp.tile(cos, (reps, 1)), jnp.tile(sin, (reps, 1))


def _row_tile(t, cap):
    tm = min(t, cap)
    assert t % tm == 0
    return tm


def _layer(x, pos, conv_buf, dn_state, moba_fn, lw, *, q_dtype):
    (w_pre_mix, w_main, w_ba, conv_w, a_log, dt_bias, dn_norm_w, w_out, w_post_mix, w_pre_mlp,
     w_up, w_down, w_post_mlp) = lw
    batch, seq, d = x.shape
    t = batch * seq
    x2 = x.reshape(t, d)
    tm = _row_tile(t, 1024)
    assert tm % seq == 0 or seq % tm == 0
    cos, sin = _rope_tables(pos, tm)
    q, k, v, qkv_d, z, ba = _in_proj(x2, w_pre_mix.reshape(1, d), w_main, w_ba, cos, sin, tm=tm, q_dtype=q_dtype)
    o_a = moba_fn(q, k, v)

    c = DN_CHUNK
    seq_pad = -(-seq // c) * c
    if seq_pad == seq:
        dn_in, n_valid = (qkv_d, z, ba), c
    else:
        assert seq < c
        padrows = lambda a: jnp.pad(a.reshape(batch, seq, -1), ((0, 0), (0, c - seq), (0, 0))).reshape(batch * c, -1)
        dn_in, n_valid = (padrows(qkv_d), padrows(z), padrows(ba)), seq
    o_d, new_state = _deltanet(*dn_in, conv_buf, dn_state, conv_w, a_log, dt_bias, dn_norm_w, batch, seq_pad, n_valid)
    if seq_pad != seq:
        o_d = o_d.reshape(batch, c, WIDTH)[:, :seq].reshape(t, WIDTH)
    keep = CONV_W - 1
    tail = qkv_d.reshape(batch, seq, -1)[:, max(0, seq - keep):]
    new_buf = tail if seq >= keep else jnp.concatenate([conv_buf, tail], axis=1)[:, -keep:]

    tm2 = _row_tile(t, 512)
    x1, h2 = _out_proj(o_a, o_d, x2, w_out, w_post_mix.reshape(1, d), w_pre_mlp.reshape(1, d), tm=tm2)
    y = _mlp(h2, x1, w_up, w_down, w_post_mlp.reshape(1, d), tm=tm2, tf=min(512, w_up.shape[1]))
    heads = (batch, seq, N_HEADS, HEAD_DIM)
    return y.reshape(batch, seq, d), k.reshape(heads), v.reshape(heads), new_buf, new_state


def kernel(x_prompt, x_sample, cache_k, cache_v, state_dn, state_conv, page_table, w_pre_mix, w_in, conv_w,
           a_log, dt_bias, dn_norm_w, w_out, w_post_mix, w_pre_mlp, w_up, w_down, w_post_mlp):
    depth = w_in.shape[0]
    batch, seq, _ = x_prompt.shape
    dec_batch, dec_seq, _ = x_sample.shape
    past = page_table.shape[1] * cache_k.shape[2]
    pos_p = jnp.arange(seq, dtype=jnp.int32)
    pos_s = past + jnp.arange(dec_seq, dtype=jnp.int32)
    n_main = 3 * WIDTH + 4 * WIDTH
    hp, hs = x_prompt, x_sample
    outs = [[] for _ in range(8)]
    for l in range(depth):
        w_ba = jnp.pad(w_in[l][:, n_main:], ((0, 0), (0, LANES - 2 * N_HEADS))).astype(BF16)
        lw = (w_pre_mix[l], w_in[l][:, :n_main].astype(BF16), w_ba, conv_w[l], a_log[l], dt_bias[l], dn_norm_w[l],
              w_out[l].astype(BF16), w_post_mix[l], w_pre_mlp[l], w_up[l].astype(BF16), w_down[l].astype(BF16),
              w_post_mlp[l])
        zero_buf = jnp.zeros((batch, CONV_W - 1, 3 * WIDTH), x_prompt.dtype)
        zero_state = jnp.zeros((batch, N_HEADS, HEAD_DIM, HEAD_DIM), state_dn.dtype)
        prompt_moba = functools.partial(_moba_prompt, batch=batch, seq=seq)
        hp, kp, vp, cvp, dnp = _layer(hp, pos_p, zero_buf, zero_state, prompt_moba, lw, q_dtype=BF16)
        sample_moba = functools.partial(_moba_sample, cache_k=cache_k, cache_v=cache_v, page_table=page_table,
                                        layer=l, dec_batch=dec_batch, dec_seq=dec_seq)
        hs, kss, vss, cvs, dns = _layer(hs, pos_s, state_conv[l], state_dn[l], sample_moba, lw, q_dtype=F32)
        for lst, val in zip(outs, (kp, vp, kss, vss, dnp, dns, cvp, cvs)):
            lst.append(val)
    return (hp, hs) + tuple(jnp.stack(o) for o in outs)
```

```python
import functools
import math

import jax
import jax.numpy as jnp
from jax import lax
from jax.experimental import pallas as pl
from jax.experimental.pallas import tpu as pltpu

F32 = jnp.float32
BF16 = jnp.bfloat16
HIGHEST = lax.Precision.HIGHEST

HEAD_DIM = 128
N_HEADS = 8
WIDTH = N_HEADS * HEAD_DIM
MOBA_BLOCK = 256
MOBA_TOPK = 3
DN_CHUNK = 64
DN_ROWS = 2
CONV_W = 4
ROPE_THETA = 10000.0
NORM_EPS = 1e-6
ATTN_SCALE = HEAD_DIM ** -0.5
LANES = 128
SUBLANES = 8
MXU_WIDTH = 256
VMEM_LIMIT = 56 * 1024 * 1024

_NT = (((1,), (1,)), ((), ()))
_TN = (((0,), (0,)), ((), ()))


def _params(*sem):
    return pltpu.CompilerParams(dimension_semantics=sem, vmem_limit_bytes=VMEM_LIMIT)


def _rms(x, w):
    return x * lax.rsqrt(jnp.mean(x * x, axis=-1, keepdims=True) + NORM_EPS) * w


def _silu(x):
    return x * jax.nn.sigmoid(x)


def _row_tile(t, cap):
    tm = min(t, cap)
    assert t % tm == 0
    return tm


def _norm_cast_body(x_ref, w_ref, h_ref):
    h_ref[...] = _rms(x_ref[...], w_ref[...]).astype(h_ref.dtype)


def _norm_cast(x, w_norm):
    t, d = x.shape
    tm = _row_tile(t, 512)
    return pl.pallas_call(
        _norm_cast_body,
        grid=(t // tm,),
        in_specs=[pl.BlockSpec((tm, d), lambda i: (i, 0)), pl.BlockSpec((1, d), lambda i: (0, 0))],
        out_specs=pl.BlockSpec((tm, d), lambda i: (i, 0)),
        out_shape=jax.ShapeDtypeStruct((t, d), BF16),
        compiler_params=_params("arbitrary"),
        name="norm_cast",
    )(x, w_norm)


def _proj_body(h_ref, w_ref, *rest, rope, tn):
    o_ref = rest[-1]
    chunk = min(tn, MXU_WIDTH)
    for c0 in range(0, tn, chunk):
        acc = jnp.dot(h_ref[...], w_ref[:, c0:c0 + chunk], preferred_element_type=F32)
        if rope:
            cos, sin = rest[0][...], rest[1][...]
            acc = jnp.concatenate(
                [acc[:, c:c + HEAD_DIM] * cos + pltpu.roll(acc[:, c:c + HEAD_DIM], HEAD_DIM // 2, 1) * sin
                 for c in range(0, chunk, HEAD_DIM)], axis=1)
        o_ref[:, c0:c0 + chunk] = acc.astype(o_ref.dtype)


def _proj(h, w, *, out_dtype, rope=None):
    t, d = h.shape
    n = w.shape[1]
    tm = _row_tile(t, 1024)
    tn = min(n, 1024)
    assert n % tn == 0
    in_specs = [pl.BlockSpec((tm, d), lambda i, j: (i, 0)), pl.BlockSpec((d, tn), lambda i, j: (0, j))]
    args = [h, w]
    if rope is not None:
        n_rope = rope[0].shape[0] // tm
        in_specs += [pl.BlockSpec((tm, HEAD_DIM), lambda i, j: (i % n_rope, 0))] * 2
        args += list(rope)
    return pl.pallas_call(
        functools.partial(_proj_body, rope=rope is not None, tn=tn),
        grid=(t // tm, n // tn),
        in_specs=in_specs,
        out_specs=pl.BlockSpec((tm, tn), lambda i, j: (i, j)),
        out_shape=jax.ShapeDtypeStruct((t, n), out_dtype),
        compiler_params=_params("arbitrary", "arbitrary"),
        name="proj_rope" if rope is not None else "proj",
    )(*args)


def _topk_mask(gate, valid, axis):
    n = gate.shape[axis]
    idx = lax.broadcasted_iota(jnp.int32, gate.shape, axis)
    g = jnp.where(valid, gate, -jnp.inf)
    sel = jnp.zeros(gate.shape, jnp.bool_)
    for _ in range(min(MOBA_TOPK, n)):
        m = jnp.max(g, axis=axis, keepdims=True)
        first = jnp.min(jnp.where(g == m, idx, n), axis=axis, keepdims=True)
        hit = idx == first
        sel = sel | hit
        g = jnp.where(hit, -jnp.inf, g)
    return sel & valid


def _moba_prompt_body(q_ref, k_ref, v_ref, o_ref, kb_scr, vt_scr, km_scr, sel_scr, m_scr, l_scr, acc_scr, *, nb):
    blk = MOBA_BLOCK
    qi = pl.program_id(1)
    heads = [slice(h * HEAD_DIM, (h + 1) * HEAD_DIM) for h in range(N_HEADS)]

    @pl.when(qi == 0)
    def _():
        for h in range(N_HEADS):
            for j in range(nb):
                rows = slice(j * blk, (j + 1) * blk)
                k = k_ref[rows, heads[h]]
                kb_scr[h, rows, :] = k.astype(BF16)
                vt_scr[h, :, rows] = v_ref[rows, heads[h]].T.astype(BF16)
                km_scr[h, j:j + 1, :] = jnp.sum(k, axis=0, keepdims=True) / blk

    own = pl.multiple_of(qi * blk, blk)
    kpos = lax.broadcasted_iota(jnp.int32, (blk, blk), 0)
    qpos = lax.broadcasted_iota(jnp.int32, (blk, blk), 1)
    block_id = lax.broadcasted_iota(jnp.int32, (nb, blk), 0)
    hs = range(N_HEADS)
    gate = [lax.dot_general(km_scr[h], q_ref[:, heads[h]].astype(F32), _NT, precision=HIGHEST,
                            preferred_element_type=F32) for h in hs]
    s = [lax.dot_general(kb_scr[h, pl.ds(own, blk), :], q_ref[:, heads[h]], _NT, preferred_element_type=F32)
         for h in hs]
    p = []
    for h in hs:
        sel_scr[h] = _topk_mask(gate[h], block_id < qi, 0).astype(F32)
        s_h = jnp.where(kpos <= qpos, s[h], -jnp.inf)
        m = jnp.max(s_h, axis=0, keepdims=True)
        p_h = jnp.exp(s_h - m)
        m_scr[h] = m
        l_scr[h] = jnp.sum(p_h, axis=0, keepdims=True)
        p.append(p_h.astype(BF16))
    for h in hs:
        acc_scr[h] = jnp.dot(vt_scr[h, :, pl.ds(own, blk)], p[h], preferred_element_type=F32)

    def past_block(j, carry):
        off = pl.multiple_of(j * blk, blk)
        s = [lax.dot_general(kb_scr[h, pl.ds(off, blk), :], q_ref[:, heads[h]], _NT, preferred_element_type=F32)
             for h in hs]
        p, alpha = [], []
        for h in hs:
            s_h = jnp.where(sel_scr[h, pl.ds(j, 1), :] > 0, s[h], -jnp.inf)
            m_old = m_scr[h]
            m_new = jnp.maximum(m_old, jnp.max(s_h, axis=0, keepdims=True))
            a_h = jnp.exp(m_old - m_new)
            p_h = jnp.exp(s_h - m_new)
            m_scr[h] = m_new
            l_scr[h] = a_h * l_scr[h] + jnp.sum(p_h, axis=0, keepdims=True)
            p.append(p_h.astype(BF16))
            alpha.append(a_h)
        pv = [jnp.dot(vt_scr[h, :, pl.ds(off, blk)], p[h], preferred_element_type=F32) for h in hs]
        for h in hs:
            acc_scr[h] = alpha[h] * acc_scr[h] + pv[h]
        return carry

    lax.fori_loop(0, qi, past_block, 0)
    for h in range(N_HEADS):
        o_ref[:, heads[h]] = (acc_scr[h] / l_scr[h]).T.astype(o_ref.dtype)


def _moba_prompt(q, k, v, batch, seq):
    assert seq % MOBA_BLOCK == 0
    nb = seq // MOBA_BLOCK
    t = batch * seq
    return pl.pallas_call(
        functools.partial(_moba_prompt_body, nb=nb),
        grid=(batch, nb),
        in_specs=[
            pl.BlockSpec((MOBA_BLOCK, WIDTH), lambda b, i: (b * nb + i, 0)),
            pl.BlockSpec((seq, WIDTH), lambda b, i: (b, 0)),
            pl.BlockSpec((seq, WIDTH), lambda b, i: (b, 0)),
        ],
        out_specs=pl.BlockSpec((MOBA_BLOCK, WIDTH), lambda b, i: (b * nb + i, 0)),
        out_shape=jax.ShapeDtypeStruct((t, WIDTH), BF16),
        scratch_shapes=[
            pltpu.VMEM((N_HEADS, seq, HEAD_DIM), BF16),
            pltpu.VMEM((N_HEADS, HEAD_DIM, seq), BF16),
            pltpu.VMEM((N_HEADS, nb, HEAD_DIM), F32),
            pltpu.VMEM((N_HEADS, nb, MOBA_BLOCK), F32),
            pltpu.VMEM((N_HEADS, 1, MOBA_BLOCK), F32),
            pltpu.VMEM((N_HEADS, 1, MOBA_BLOCK), F32),
            pltpu.VMEM((N_HEADS, HEAD_DIM, MOBA_BLOCK), F32),
        ],
        compiler_params=_params("arbitrary", "arbitrary"),
        name="moba_prompt",
    )(q, k, v)


def _page_sum_body(pt_ref, *refs, n_in, ppb):
    del pt_ref
    out_ref = refs[n_in]
    for i in range(n_in // ppb):
        tot = jnp.sum(refs[i * ppb][...], axis=0)
        for r in range(1, ppb):
            tot = tot + jnp.sum(refs[i * ppb + r][...], axis=0)
        out_ref[i] = tot / MOBA_BLOCK


def _block_means(cache_k, page_table_flat, layer, dec_batch, n_pages):
    page = cache_k.shape[2]
    ppb = MOBA_BLOCK // page
    assert n_pages % ppb == 0
    per_step = ppb * SUBLANES
    while n_pages % per_step:
        per_step //= 2
    assert per_step % ppb == 0
    steps = n_pages // per_step
    nb = n_pages // ppb

    def spec(r):
        return pl.BlockSpec((None, None, page, N_HEADS, HEAD_DIM),
                            lambda b, p, pt, r=r: (layer, pt[b * n_pages + p * per_step + r], 0, 0, 0))

    return pl.pallas_call(
        functools.partial(_page_sum_body, n_in=per_step, ppb=ppb),
        grid_spec=pltpu.PrefetchScalarGridSpec(
            num_scalar_prefetch=1,
            grid=(dec_batch, steps),
            in_specs=[spec(r) for r in range(per_step)],
            out_specs=pl.BlockSpec((None, per_step // ppb, N_HEADS, HEAD_DIM), lambda b, p, pt: (b, p, 0, 0)),
        ),
        out_shape=jax.ShapeDtypeStruct((dec_batch, nb, N_HEADS, HEAD_DIM), F32),
        compiler_params=_params("arbitrary", "arbitrary"),
        name="moba_block_means",
    )(page_table_flat, *([cache_k] * per_step))


def _sample_select_body(q_ref, km_ref, idx_ref, *, dec_seq):
    for h in range(N_HEADS):
        q = q_ref[:, h * HEAD_DIM:(h + 1) * HEAD_DIM]
        gate = lax.dot_general(q, km_ref[h], _NT, precision=HIGHEST, preferred_element_type=F32)
        nb = gate.shape[1]
        idx = lax.broadcasted_iota(jnp.int32, gate.shape, 1)
        lane = lax.broadcasted_iota(jnp.int32, (dec_seq, LANES), 1)
        out = jnp.zeros((dec_seq, LANES), jnp.int32)
        g = gate
        for r in range(MOBA_TOPK):
            m = jnp.max(g, axis=1, keepdims=True)
            first = jnp.min(jnp.where(g == m, idx, nb), axis=1, keepdims=True)
            out = jnp.where(lane == r, first, out)
            g = jnp.where(idx == first, -jnp.inf, g)
        idx_ref[h] = out


def _sample_select(q, kmean_t, dec_batch, dec_seq):
    nb = kmean_t.shape[2]
    return pl.pallas_call(
        functools.partial(_sample_select_body, dec_seq=dec_seq),
        grid=(dec_batch,),
        in_specs=[
            pl.BlockSpec((dec_seq, WIDTH), lambda b: (b, 0)),
            pl.BlockSpec((None, N_HEADS, nb, HEAD_DIM), lambda b: (b, 0, 0, 0)),
        ],
        out_specs=pl.BlockSpec((None, N_HEADS, dec_seq, LANES), lambda b: (b, 0, 0, 0)),
        out_shape=jax.ShapeDtypeStruct((dec_batch, N_HEADS, dec_seq, LANES), jnp.int32),
        compiler_params=_params("arbitrary"),
        name="moba_sample_select",
    )(q, kmean_t)


def _sample_attn_body(pages_ref, q_ref, kn_ref, vn_ref, ck_hbm, cv_hbm, o_ref, kbuf, vbuf, sem,
                      *, layer, dec_seq, ppb, page):
    n_sel = MOBA_TOPK * ppb
    step = pl.program_id(0) * N_HEADS + pl.program_id(1)
    n_steps = pl.num_programs(0) * N_HEADS
    slot = step % 2

    def slab_copies(at_step, at_slot, qi, s):
        head = at_step % N_HEADS
        phys = pages_ref[(at_step * dec_seq + qi) * n_sel + s]
        dst = pl.ds(s * page, page)
        return (pltpu.make_async_copy(ck_hbm.at[layer, phys, :, head, :], kbuf.at[at_slot, qi, dst, :], sem.at[0, at_slot]),
                pltpu.make_async_copy(cv_hbm.at[layer, phys, :, head, :], vbuf.at[at_slot, qi, dst, :], sem.at[1, at_slot]))

    def for_all_slabs(at_step, at_slot, action):
        for qi in range(dec_seq):
            for s in range(n_sel):
                for c in slab_copies(at_step, at_slot, qi, s):
                    action(c)

    @pl.when(step == 0)
    def _():
        for_all_slabs(step, slot, lambda c: c.start())

    @pl.when(step + 1 < n_steps)
    def _():
        for_all_slabs(step + 1, 1 - slot, lambda c: c.start())

    for_all_slabs(step, slot, lambda c: c.wait())

    qb = q_ref[...].astype(BF16)
    vn = vn_ref[...]
    s_own = lax.dot_general(kn_ref[...].astype(BF16), qb, _NT, preferred_element_type=F32)
    kpos = lax.broadcasted_iota(jnp.int32, s_own.shape, 0)
    qpos = lax.broadcasted_iota(jnp.int32, s_own.shape, 1)
    s_own = jnp.where(kpos <= qpos, s_own, -jnp.inf)
    rows = []
    for qi in range(dec_seq):
        s_all = lax.dot_general(kbuf[slot, qi].astype(BF16), qb, _NT, preferred_element_type=F32)
        s_sel = s_all[:, qi:qi + 1]
        s_new = s_own[:, qi:qi + 1]
        m = jnp.maximum(jnp.max(s_sel, axis=0, keepdims=True), jnp.max(s_new, axis=0, keepdims=True))
        p_sel = jnp.exp(s_sel - m)
        p_new = jnp.exp(s_new - m)
        l = jnp.sum(p_sel, axis=0, keepdims=True) + jnp.sum(p_new, axis=0, keepdims=True)
        pv = jnp.sum(p_sel * vbuf[slot, qi], axis=0, keepdims=True) + jnp.sum(p_new * vn, axis=0, keepdims=True)
        rows.append(pv / l)
    o_ref[...] = jnp.concatenate(rows, axis=0).astype(o_ref.dtype)


def _sample_attn(pages_flat, q, k_new, v_new, cache_k, cache_v, layer, dec_batch, dec_seq):
    page = cache_k.shape[2]
    ppb = MOBA_BLOCK // page
    n_keys = MOBA_TOPK * MOBA_BLOCK
    row = lambda b, h, pg: (b, h)
    return pl.pallas_call(
        functools.partial(_sample_attn_body, layer=layer, dec_seq=dec_seq, ppb=ppb, page=page),
        grid_spec=pltpu.PrefetchScalarGridSpec(
            num_scalar_prefetch=1,
            grid=(dec_batch, N_HEADS),
            in_specs=[
                pl.BlockSpec((dec_seq, HEAD_DIM), row),
                pl.BlockSpec((dec_seq, HEAD_DIM), row),
                pl.BlockSpec((dec_seq, HEAD_DIM), row),
                pl.BlockSpec(memory_space=pl.ANY),
                pl.BlockSpec(memory_space=pl.ANY),
            ],
            out_specs=pl.BlockSpec((dec_seq, HEAD_DIM), row),
            scratch_shapes=[
                pltpu.VMEM((2, dec_seq, n_keys, HEAD_DIM), F32),
                pltpu.VMEM((2, dec_seq, n_keys, HEAD_DIM), F32),
                pltpu.SemaphoreType.DMA((2, 2)),
            ],
        ),
        out_shape=jax.ShapeDtypeStruct((dec_batch * dec_seq, WIDTH), F32),
        compiler_params=_params("arbitrary", "arbitrary"),
        name="moba_sample_attn",
    )(pages_flat, q, k_new, v_new, cache_k, cache_v)


def _moba_sample(q, k_new, v_new, cache_k, cache_v, page_table, layer, dec_batch, dec_seq):
    n_pages = page_table.shape[1]
    page = cache_k.shape[2]
    ppb = MOBA_BLOCK // page
    past = n_pages * page
    assert past % MOBA_BLOCK == 0 and dec_seq <= MOBA_BLOCK and past // MOBA_BLOCK >= MOBA_TOPK
    assert dec_seq % SUBLANES == 0
    pt_flat = page_table.reshape(-1)
    kmean = _block_means(cache_k, pt_flat, layer, dec_batch, n_pages)
    idx = _sample_select(q, jnp.transpose(kmean, (0, 2, 1, 3)), dec_batch, dec_seq)[..., :MOBA_TOPK]
    logical = idx[..., None] * ppb + jnp.arange(ppb, dtype=jnp.int32)
    pages = jnp.take_along_axis(page_table, logical.reshape(dec_batch, -1), axis=1)
    o = _sample_attn(pages.reshape(-1), q, k_new, v_new, cache_k, cache_v, layer, dec_batch, dec_seq)
    return o.astype(BF16)


def _deltanet_body(x_ref, z_ref, ba_ref, cbuf_ref, cw_ref, alog_ref, dtb_ref, nw_ref, s0_ref,
                   o_ref, sout_ref, xp_scr, s_scr, *, n_valid, rows):
    c = DN_CHUNK
    step = pl.program_id(1)

    @pl.when(step == 0)
    def _():
        s_scr[...] = s0_ref[...]
        xp_scr[:, 0:SUBLANES, :] = cbuf_ref[...]

    @pl.when(step > 0)
    def _():
        xp_scr[:, 0:SUBLANES, :] = xp_scr[:, c:c + SUBLANES, :]

    xp_scr[:, SUBLANES:, :] = x_ref[...]

    def l2n(a):
        return a * lax.rsqrt(jnp.sum(a * a, axis=-1, keepdims=True) + NORM_EPS)

    def mm(a, b, dims=None):
        a, b = a.astype(BF16), b.astype(BF16)
        if dims is None:
            return jnp.dot(a, b, preferred_element_type=F32)
        return lax.dot_general(a, b, dims, preferred_element_type=F32)

    ri = lax.broadcasted_iota(jnp.int32, (c, c), 0)
    ci = lax.broadcasted_iota(jnp.int32, (c, c), 1)
    causal = ri >= ci
    strict = ri > ci
    eye = (ri == ci).astype(F32)
    tril_ones = causal.astype(F32)
    first = SUBLANES - (CONV_W - 1)

    for r in range(rows):
        def conv(col):
            lanes = slice(col, col + HEAD_DIM)
            y = xp_scr[r, first:first + c, lanes] * cw_ref[0:1, lanes]
            for j in range(1, CONV_W):
                y = y + xp_scr[r, first + j:first + j + c, lanes] * cw_ref[j:j + 1, lanes]
            return _silu(y)

        ba = ba_ref[r]
        beta = jax.nn.sigmoid(ba)
        xs = ba + dtb_ref[...]
        g = -jnp.exp(alog_ref[...]) * (jnp.maximum(xs, 0.0) + jnp.log1p(jnp.exp(-jnp.abs(xs))))
        if n_valid < c:
            live = lax.broadcasted_iota(jnp.int32, ba.shape, 0) < n_valid
            beta = jnp.where(live, beta, 0.0)
            g = jnp.where(live, g, 0.0)
        gc = jnp.dot(tril_ones, g, precision=HIGHEST, preferred_element_type=F32)
        gct = gc.T
        eg = jnp.exp(gc)
        g_last = gc[c - 1:c, :]
        kdf = jnp.exp(g_last - gc)
        g_tot = jnp.exp(g_last)

        hs = range(N_HEADS)
        col = lambda a, h: a[:, N_HEADS + h:N_HEADS + h + 1]
        qn = [l2n(conv(h * HEAD_DIM)) * (HEAD_DIM ** -0.5) for h in hs]
        kn = [l2n(conv(WIDTH + h * HEAD_DIM)) for h in hs]
        kb = [kn[h] * beta[:, h:h + 1] for h in hs]
        kk_qk = [mm(jnp.concatenate([kb[h], qn[h]], axis=0), kn[h], _NT) for h in hs]
        decay = [jnp.where(causal, jnp.exp(jnp.where(causal, col(gc, h) - gct[N_HEADS + h:N_HEADS + h + 1, :], 0.0)), 0.0)
                 for h in hs]
        a_qk = [jnp.where(causal, kk_qk[h][c:] * decay[h], 0.0) for h in hs]
        pw = [jnp.where(strict, -kk_qk[h][:c] * decay[h], 0.0) for h in hs]
        t_inv = [eye + pw[h] for h in hs]
        pw = [mm(pw[h], pw[h]) for h in hs]
        for _ in range(int(math.log2(c)) - 2):
            both = [mm(jnp.concatenate([t_inv[h], pw[h]], axis=0), pw[h]) for h in hs]
            t_inv = [t_inv[h] + both[h][:c] for h in hs]
            pw = [both[h][c:] for h in hs]
        t_inv = [t_inv[h] + mm(t_inv[h], pw[h]) for h in hs]
        vb_kbe = [jnp.concatenate([conv(2 * WIDTH + h * HEAD_DIM) * beta[:, h:h + 1], kb[h] * col(eg, h)], axis=1)
                  for h in hs]
        uw = [mm(t_inv[h], vb_kbe[h]) for h in hs]
        s_old = [s_scr[r, h] for h in hs]
        sq = [mm(jnp.concatenate([uw[h][:, HEAD_DIM:], qn[h] * col(eg, h)], axis=0), s_old[h]) for h in hs]
        v_new = [uw[h][:, :HEAD_DIM] - sq[h][:c] for h in hs]
        o = [sq[h][c:] + mm(a_qk[h], v_new[h]) for h in hs]
        kv = [mm(kn[h] * col(kdf, h), v_new[h], _TN) for h in hs]
        for h in hs:
            s_scr[r, h] = s_old[h] * col(g_tot, h) + kv[h]
            lanes = slice(h * HEAD_DIM, (h + 1) * HEAD_DIM)
            o_ref[r, :, lanes] = (_rms(o[h], nw_ref[...]) * _silu(z_ref[r, :, lanes])).astype(o_ref.dtype)

    @pl.when(step == pl.num_programs(1) - 1)
    def _():
        sout_ref[...] = s_scr[...]


def _deltanet(qkv, z, ba, conv_buf, state0, conv_w, a_log, dt_bias, norm_w, batch, seq, n_valid, rows):
    c = DN_CHUNK
    assert seq % c == 0 and (n_valid == c or seq == c) and batch % rows == 0
    n_chunks = seq // c
    cw3 = 3 * WIDTH
    cbuf = jnp.pad(conv_buf, ((0, 0), (SUBLANES - (CONV_W - 1), 0), (0, 0)))
    pad = (0, LANES - 2 * N_HEADS)
    alog = jnp.pad(jnp.concatenate([jnp.zeros_like(a_log), a_log]), pad).reshape(1, LANES)
    dtb = jnp.pad(jnp.concatenate([jnp.zeros_like(dt_bias), dt_bias]), pad).reshape(1, LANES)
    tok = lambda b, i: (b, i, 0)
    per_seq3 = lambda b, i: (b, 0, 0)
    per_seq4 = lambda b, i: (b, 0, 0, 0)
    fixed = lambda b, i: (0, 0)
    o, s_out = pl.pallas_call(
        functools.partial(_deltanet_body, n_valid=n_valid, rows=rows),
        grid=(batch // rows, n_chunks),
        in_specs=[
            pl.BlockSpec((rows, c, cw3), tok),
            pl.BlockSpec((rows, c, WIDTH), tok),
            pl.BlockSpec((rows, c, LANES), tok),
            pl.BlockSpec((rows, SUBLANES, cw3), per_seq3),
            pl.BlockSpec((CONV_W, cw3), fixed),
            pl.BlockSpec((1, LANES), fixed),
            pl.BlockSpec((1, LANES), fixed),
            pl.BlockSpec((1, HEAD_DIM), fixed),
            pl.BlockSpec((rows, N_HEADS, HEAD_DIM, HEAD_DIM), per_seq4),
        ],
        out_specs=[
            pl.BlockSpec((rows, c, WIDTH), tok),
            pl.BlockSpec((rows, N_HEADS, HEAD_DIM, HEAD_DIM), per_seq4),
        ],
        out_shape=[
            jax.ShapeDtypeStruct((batch, seq, WIDTH), BF16),
            jax.ShapeDtypeStruct((batch, N_HEADS, HEAD_DIM, HEAD_DIM), F32),
        ],
        scratch_shapes=[
            pltpu.VMEM((rows, c + SUBLANES, cw3), F32),
            pltpu.VMEM((rows, N_HEADS, HEAD_DIM, HEAD_DIM), F32),
        ],
        compiler_params=_params("arbitrary", "arbitrary"),
        name="deltanet",
    )(qkv.reshape(batch, seq, cw3), z.reshape(batch, seq, WIDTH), ba.reshape(batch, seq, LANES),
      cbuf, conv_w, alog, dtb, norm_w.reshape(1, HEAD_DIM), state0)
    return o.reshape(batch * seq, WIDTH), s_out


def _out_proj_body(oa_ref, od_ref, x_ref, w_ref, wpost_ref, wpre_ref, x1_ref, h2_ref):
    mix = (jnp.dot(oa_ref[...], w_ref[0:WIDTH, :], preferred_element_type=F32)
           + jnp.dot(od_ref[...], w_ref[WIDTH:, :], preferred_element_type=F32))
    x1 = x_ref[...] + _rms(mix, wpost_ref[...])
    x1_ref[...] = x1
    h2_ref[...] = _rms(x1, wpre_ref[...]).astype(h2_ref.dtype)


def _out_proj(o_a, o_d, x, w_out, w_post, w_pre):
    t, d = x.shape
    tm = _row_tile(t, 512)
    row = lambda i: (i, 0)
    fixed = lambda i: (0, 0)
    return pl.pallas_call(
        _out_proj_body,
        grid=(t // tm,),
        in_specs=[
            pl.BlockSpec((tm, WIDTH), row),
            pl.BlockSpec((tm, WIDTH), row),
            pl.BlockSpec((tm, d), row),
            pl.BlockSpec((2 * WIDTH, d), fixed),
            pl.BlockSpec((1, d), fixed),
            pl.BlockSpec((1, d), fixed),
        ],
        out_specs=[pl.BlockSpec((tm, d), row), pl.BlockSpec((tm, d), row)],
        out_shape=[jax.ShapeDtypeStruct((t, d), F32), jax.ShapeDtypeStruct((t, d), BF16)],
        compiler_params=_params("arbitrary"),
        name="out_proj",
    )(o_a, o_d, x, w_out, w_post, w_pre)


def _mlp_body(h_ref, x1_ref, wu_ref, wd_ref, wpost_ref, y_ref):
    f = pl.program_id(1)

    @pl.when(f == 0)
    def _():
        y_ref[...] = jnp.zeros_like(y_ref)

    a = jnp.dot(h_ref[...], wu_ref[...], preferred_element_type=F32)
    a = jnp.square(jnp.maximum(a, 0.0)).astype(BF16)
    y_ref[...] += jnp.dot(a, wd_ref[...], preferred_element_type=F32)

    @pl.when(f == pl.num_programs(1) - 1)
    def _():
        y_ref[...] = x1_ref[...] + _rms(y_ref[...], wpost_ref[...])


def _mlp(h2, x1, w_up, w_down, w_post):
    t, d = x1.shape
    d_ff = w_up.shape[1]
    tm = _row_tile(t, 512)
    tf = min(1024, d_ff)
    assert d_ff % tf == 0
    return pl.pallas_call(
        _mlp_body,
        grid=(t // tm, d_ff // tf),
        in_specs=[
            pl.BlockSpec((tm, d), lambda i, f: (i, 0)),
            pl.BlockSpec((tm, d), lambda i, f: (i, 0)),
            pl.BlockSpec((d, tf), lambda i, f: (0, f)),
            pl.BlockSpec((tf, d), lambda i, f: (f, 0)),
            pl.BlockSpec((1, d), lambda i, f: (0, 0)),
        ],
        out_specs=pl.BlockSpec((tm, d), lambda i, f: (i, 0)),
        out_shape=jax.ShapeDtypeStruct((t, d), F32),
        compiler_params=_params("arbitrary", "arbitrary"),
        name="mlp",
    )(h2, x1, w_up, w_down, w_post)


def _rope_tables(pos, rows, scale):
    half = HEAD_DIM // 2
    inv_freq = 1.0 / (ROPE_THETA ** (jnp.arange(half, dtype=F32) / half))
    ang = pos.astype(F32)[:, None] * inv_freq[None, :]
    cos = jnp.concatenate([jnp.cos(ang), jnp.cos(ang)], axis=1) * scale
    sin = jnp.concatenate([-jnp.sin(ang), jnp.sin(ang)], axis=1) * scale
    reps = max(1, rows // pos.shape[0])
    return jnp.tile(cos, (reps, 1)), jnp.tile(sin, (reps, 1))


def _layer(x, pos, conv_buf, dn_state, moba_fn, lw, *, q_dtype):
    (w_pre_mix, w_q, w_k, w_v, w_d, w_z, w_ba, conv_w, a_log, dt_bias, dn_norm_w, w_out, w_post_mix, w_pre_mlp,
     w_up, w_down, w_post_mlp) = lw
    batch, seq, d = x.shape
    t = batch * seq
    x2 = x.reshape(t, d)
    tm = _row_tile(t, 1024)
    assert tm % seq == 0 or seq % tm == 0
    h = _norm_cast(x2, w_pre_mix.reshape(1, d))
    q = _proj(h, w_q, out_dtype=q_dtype, rope=_rope_tables(pos, tm, ATTN_SCALE))
    k = _proj(h, w_k, out_dtype=F32, rope=_rope_tables(pos, tm, 1.0))
    v = _proj(h, w_v, out_dtype=F32)
    qkv_d = _proj(h, w_d, out_dtype=F32)
    z = _proj(h, w_z, out_dtype=F32)
    ba = _proj(h, w_ba, out_dtype=F32)
    o_a = moba_fn(q, k, v)

    c = DN_CHUNK
    seq_pad = -(-seq // c) * c
    if seq_pad == seq:
        dn_in, n_valid = (qkv_d, z, ba), c
    else:
        assert seq < c
        padrows = lambda a: jnp.pad(a.reshape(batch, seq, -1), ((0, 0), (0, c - seq), (0, 0))).reshape(batch * c, -1)
        dn_in, n_valid = (padrows(qkv_d), padrows(z), padrows(ba)), seq
    o_d, new_state = _deltanet(*dn_in, conv_buf, dn_state, conv_w, a_log, dt_bias, dn_norm_w, batch, seq_pad, n_valid,
                               rows=DN_ROWS)
    if seq_pad != seq:
        o_d = o_d.reshape(batch, c, WIDTH)[:, :seq].reshape(t, WIDTH)
    keep = CONV_W - 1
    tail = qkv_d.reshape(batch, seq, -1)[:, max(0, seq - keep):]
    new_buf = tail if seq >= keep else jnp.concatenate([conv_buf, tail], axis=1)[:, -keep:]

    x1, h2 = _out_proj(o_a, o_d, x2, w_out, w_post_mix.reshape(1, d), w_pre_mlp.reshape(1, d))
    y = _mlp(h2, x1, w_up, w_down, w_post_mlp.reshape(1, d))
    heads = (batch, seq, N_HEADS, HEAD_DIM)
    return y.reshape(batch, seq, d), k.reshape(heads), v.reshape(heads), new_buf, new_state


def kernel(x_prompt, x_sample, cache_k, cache_v, state_dn, state_conv, page_table, w_pre_mix, w_in, conv_w,
           a_log, dt_bias, dn_norm_w, w_out, w_post_mix, w_pre_mlp, w_up, w_down, w_post_mlp):
    depth = w_in.shape[0]
    batch, seq, _ = x_prompt.shape
    dec_batch, dec_seq, _ = x_sample.shape
    past = page_table.shape[1] * cache_k.shape[2]
    pos_p = jnp.arange(seq, dtype=jnp.int32)
    pos_s = past + jnp.arange(dec_seq, dtype=jnp.int32)
    cuts = [0, WIDTH, 2 * WIDTH, 3 * WIDTH, 6 * WIDTH, 7 * WIDTH]
    hp, hs = x_prompt, x_sample
    outs = [[] for _ in range(8)]
    for l in range(depth):
        w_slices = [w_in[l][:, a:b].astype(BF16) for a, b in zip(cuts[:-1], cuts[1:])]
        w_ba = jnp.pad(w_in[l][:, cuts[-1]:], ((0, 0), (0, LANES - 2 * N_HEADS))).astype(BF16)
        lw = (w_pre_mix[l], *w_slices, w_ba, conv_w[l], a_log[l], dt_bias[l], dn_norm_w[l],
              w_out[l].astype(BF16), w_post_mix[l], w_pre_mlp[l], w_up[l].astype(BF16), w_down[l].astype(BF16),
              w_post_mlp[l])
        zero_buf = jnp.zeros((batch, CONV_W - 1, 3 * WIDTH), x_prompt.dtype)
        zero_state = jnp.zeros((batch, N_HEADS, HEAD_DIM, HEAD_DIM), state_dn.dtype)
        prompt_moba = functools.partial(_moba_prompt, batch=batch, seq=seq)
        hp, kp, vp, cvp, dnp = _layer(hp, pos_p, zero_buf, zero_state, prompt_moba, lw, q_dtype=BF16)
        sample_moba = functools.partial(_moba_sample, cache_k=cache_k, cache_v=cache_v, page_table=page_table,
                                        layer=l, dec_batch=dec_batch, dec_seq=dec_seq)
        hs, kss, vss, cvs, dns = _layer(hs, pos_s, state_conv[l], state_dn[l], sample_moba, lw, q_dtype=F32)
        for lst, val in zip(outs, (kp, vp, kss, vss, dnp, dns, cvp, cvs)):
            lst.append(val)
    return (hp, hs) + tuple(jnp.stack(o) for o in outs)
```

```python
import functools
import math

import jax
import jax.numpy as jnp
from jax import lax
from jax.experimental import pallas as pl
from jax.experimental.pallas import tpu as pltpu

F32 = jnp.float32
BF16 = jnp.bfloat16
HIGHEST = lax.Precision.HIGHEST

HEAD_DIM = 128
N_HEADS = 8
WIDTH = N_HEADS * HEAD_DIM
MOBA_BLOCK = 256
MOBA_TOPK = 3
DN_CHUNK = 64
DN_ROWS = 2
CONV_W = 4
ROPE_THETA = 10000.0
NORM_EPS = 1e-6
ATTN_SCALE = HEAD_DIM ** -0.5
LANES = 128
SUBLANES = 8
MXU_WIDTH = 256
VMEM_LIMIT = 56 * 1024 * 1024

_NT = (((1,), (1,)), ((), ()))
_TN = (((0,), (0,)), ((), ()))


def _params(*sem):
    return pltpu.CompilerParams(dimension_semantics=sem, vmem_limit_bytes=VMEM_LIMIT)


def _rms(x, w):
    return x * lax.rsqrt(jnp.mean(x * x, axis=-1, keepdims=True) + NORM_EPS) * w


def _silu(x):
    return x * jax.nn.sigmoid(x)


def _row_tile(t, cap):
    tm = min(t, cap)
    assert t % tm == 0
    return tm


def _norm_cast_body(x_ref, w_ref, h_ref):
    h_ref[...] = _rms(x_ref[...], w_ref[...]).astype(h_ref.dtype)


def _norm_cast(x, w_norm):
    t, d = x.shape
    tm = _row_tile(t, 512)
    return pl.pallas_call(
        _norm_cast_body,
        grid=(t // tm,),
        in_specs=[pl.BlockSpec((tm, d), lambda i: (i, 0)), pl.BlockSpec((1, d), lambda i: (0, 0))],
        out_specs=pl.BlockSpec((tm, d), lambda i: (i, 0)),
        out_shape=jax.ShapeDtypeStruct((t, d), BF16),
        compiler_params=_params("arbitrary"),
        name="norm_cast",
    )(x, w_norm)


def _proj_body(h_ref, w_ref, *rest, rope, tn):
    o_ref = rest[-1]
    for c0 in range(0, tn, MXU_WIDTH):
        chunk = min(MXU_WIDTH, tn - c0)
        acc = jnp.dot(h_ref[...], w_ref[:, c0:c0 + chunk], preferred_element_type=F32)
        if rope:
            cos, sin = rest[0][...], rest[1][...]
            acc = jnp.concatenate(
                [acc[:, c:c + HEAD_DIM] * cos + pltpu.roll(acc[:, c:c + HEAD_DIM], HEAD_DIM // 2, 1) * sin
                 for c in range(0, chunk, HEAD_DIM)], axis=1)
        o_ref[:, c0:c0 + chunk] = acc.astype(o_ref.dtype)


def _proj(h, w, *, out_dtype, rope=None):
    t, d = h.shape
    n = w.shape[1]
    tm = _row_tile(t, 1024)
    tn = n if n <= 1536 else 1024
    assert n % tn == 0 and tn % LANES == 0
    in_specs = [pl.BlockSpec((tm, d), lambda i, j: (i, 0)), pl.BlockSpec((d, tn), lambda i, j: (0, j))]
    args = [h, w]
    if rope is not None:
        n_rope = rope[0].shape[0] // tm
        in_specs += [pl.BlockSpec((tm, HEAD_DIM), lambda i, j: (i % n_rope, 0))] * 2
        args += list(rope)
    return pl.pallas_call(
        functools.partial(_proj_body, rope=rope is not None, tn=tn),
        grid=(t // tm, n // tn),
        in_specs=in_specs,
        out_specs=pl.BlockSpec((tm, tn), lambda i, j: (i, j)),
        out_shape=jax.ShapeDtypeStruct((t, n), out_dtype),
        compiler_params=_params("arbitrary", "arbitrary"),
        name="proj_rope" if rope is not None else "proj",
    )(*args)


def _topk_mask(gate, valid, axis):
    n = gate.shape[axis]
    idx = lax.broadcasted_iota(jnp.int32, gate.shape, axis)
    g = jnp.where(valid, gate, -jnp.inf)
    sel = jnp.zeros(gate.shape, jnp.bool_)
    for _ in range(min(MOBA_TOPK, n)):
        m = jnp.max(g, axis=axis, keepdims=True)
        first = jnp.min(jnp.where(g == m, idx, n), axis=axis, keepdims=True)
        hit = idx == first
        sel = sel | hit
        g = jnp.where(hit, -jnp.inf, g)
    return sel & valid


def _moba_prompt_body(q_ref, k_ref, v_ref, o_ref, kb_scr, vt_scr, km_scr, sel_scr, m_scr, l_scr, acc_scr, *, nb):
    blk = MOBA_BLOCK
    qi = pl.program_id(1)
    heads = [slice(h * HEAD_DIM, (h + 1) * HEAD_DIM) for h in range(N_HEADS)]

    @pl.when(qi == 0)
    def _():
        for h in range(N_HEADS):
            for j in range(nb):
                rows = slice(j * blk, (j + 1) * blk)
                k = k_ref[rows, heads[h]]
                kb_scr[h, rows, :] = k.astype(BF16)
                vt_scr[h, :, rows] = v_ref[rows, heads[h]].T.astype(BF16)
                km_scr[h, j:j + 1, :] = jnp.sum(k, axis=0, keepdims=True) / blk

    own = pl.multiple_of(qi * blk, blk)
    kpos = lax.broadcasted_iota(jnp.int32, (blk, blk), 0)
    qpos = lax.broadcasted_iota(jnp.int32, (blk, blk), 1)
    block_id = lax.broadcasted_iota(jnp.int32, (nb, blk), 0)
    hs = range(N_HEADS)
    gate = [lax.dot_general(km_scr[h], q_ref[:, heads[h]].astype(F32), _NT, precision=HIGHEST,
                            preferred_element_type=F32) for h in hs]
    s = [lax.dot_general(kb_scr[h, pl.ds(own, blk), :], q_ref[:, heads[h]], _NT, preferred_element_type=F32)
         for h in hs]
    p = []
    for h in hs:
        sel_scr[h] = _topk_mask(gate[h], block_id < qi, 0).astype(F32)
        s_h = jnp.where(kpos <= qpos, s[h], -jnp.inf)
        m = jnp.max(s_h, axis=0, keepdims=True)
        p_h = jnp.exp(s_h - m)
        m_scr[h] = m
        l_scr[h] = jnp.sum(p_h, axis=0, keepdims=True)
        p.append(p_h.astype(BF16))
    for h in hs:
        acc_scr[h] = jnp.dot(vt_scr[h, :, pl.ds(own, blk)], p[h], preferred_element_type=F32)

    def past_block(j, carry):
        off = pl.multiple_of(j * blk, blk)
        s = [lax.dot_general(kb_scr[h, pl.ds(off, blk), :], q_ref[:, heads[h]], _NT, preferred_element_type=F32)
             for h in hs]
        p, alpha = [], []
        for h in hs:
            s_h = jnp.where(sel_scr[h, pl.ds(j, 1), :] > 0, s[h], -jnp.inf)
            m_old = m_scr[h]
            m_new = jnp.maximum(m_old, jnp.max(s_h, axis=0, keepdims=True))
            a_h = jnp.exp(m_old - m_new)
            p_h = jnp.exp(s_h - m_new)
            m_scr[h] = m_new
            l_scr[h] = a_h * l_scr[h] + jnp.sum(p_h, axis=0, keepdims=True)
            p.append(p_h.astype(BF16))
            alpha.append(a_h)
        pv = [jnp.dot(vt_scr[h, :, pl.ds(off, blk)], p[h], preferred_element_type=F32) for h in hs]
        for h in hs:
            acc_scr[h] = alpha[h] * acc_scr[h] + pv[h]
        return carry

    lax.fori_loop(0, qi, past_block, 0)
    for h in range(N_HEADS):
        o_ref[:, heads[h]] = (acc_scr[h] / l_scr[h]).T.astype(o_ref.dtype)


def _moba_prompt(q, k, v, batch, seq):
    assert seq % MOBA_BLOCK == 0
    nb = seq // MOBA_BLOCK
    t = batch * seq
    return pl.pallas_call(
        functools.partial(_moba_prompt_body, nb=nb),
        grid=(batch, nb),
        in_specs=[
            pl.BlockSpec((MOBA_BLOCK, WIDTH), lambda b, i: (b * nb + i, 0)),
            pl.BlockSpec((seq, WIDTH), lambda b, i: (b, 0)),
            pl.BlockSpec((seq, WIDTH), lambda b, i: (b, 0)),
        ],
        out_specs=pl.BlockSpec((MOBA_BLOCK, WIDTH), lambda b, i: (b * nb + i, 0)),
        out_shape=jax.ShapeDtypeStruct((t, WIDTH), BF16),
        scratch_shapes=[
            pltpu.VMEM((N_HEADS, seq, HEAD_DIM), BF16),
            pltpu.VMEM((N_HEADS, HEAD_DIM, seq), BF16),
            pltpu.VMEM((N_HEADS, nb, HEAD_DIM), F32),
            pltpu.VMEM((N_HEADS, nb, MOBA_BLOCK), F32),
            pltpu.VMEM((N_HEADS, 1, MOBA_BLOCK), F32),
            pltpu.VMEM((N_HEADS, 1, MOBA_BLOCK), F32),
            pltpu.VMEM((N_HEADS, HEAD_DIM, MOBA_BLOCK), F32),
        ],
        compiler_params=_params("arbitrary", "arbitrary"),
        name="moba_prompt",
    )(q, k, v)


def _sum_pages(page_refs, out_ref, ppb):
    for i in range(len(page_refs) // ppb):
        tot = jnp.sum(page_refs[i * ppb][...], axis=0)
        for r in range(1, ppb):
            tot = tot + jnp.sum(page_refs[i * ppb + r][...], axis=0)
        out_ref[i] = tot / MOBA_BLOCK


def _page_sum_body(pt_ref, *refs, n_in, ppb):
    del pt_ref
    _sum_pages(refs[:n_in], refs[n_in], ppb)


def _block_means(cache_k, page_table_flat, layer, dec_batch, n_pages):
    page = cache_k.shape[2]
    ppb = MOBA_BLOCK // page
    assert n_pages % ppb == 0
    per_step = ppb * SUBLANES
    while n_pages % per_step:
        per_step //= 2
    assert per_step % ppb == 0
    steps = n_pages // per_step
    nb = n_pages // ppb

    def spec(r):
        return pl.BlockSpec((None, None, page, N_HEADS, HEAD_DIM),
                            lambda b, p, pt, r=r: (layer, pt[b * n_pages + p * per_step + r], 0, 0, 0))

    return pl.pallas_call(
        functools.partial(_page_sum_body, n_in=per_step, ppb=ppb),
        grid_spec=pltpu.PrefetchScalarGridSpec(
            num_scalar_prefetch=1,
            grid=(dec_batch, steps),
            in_specs=[spec(r) for r in range(per_step)],
            out_specs=pl.BlockSpec((None, per_step // ppb, N_HEADS, HEAD_DIM), lambda b, p, pt: (b, p, 0, 0)),
        ),
        out_shape=jax.ShapeDtypeStruct((dec_batch, nb, N_HEADS, HEAD_DIM), F32),
        compiler_params=_params("arbitrary", "arbitrary"),
        name="moba_block_means",
    )(page_table_flat, *([cache_k] * per_step))


def _sample_select_body(q_ref, km_ref, idx_ref, *, dec_seq):
    for h in range(N_HEADS):
        q = q_ref[:, h * HEAD_DIM:(h + 1) * HEAD_DIM]
        gate = lax.dot_general(q, km_ref[:, h, :], _NT, precision=HIGHEST, preferred_element_type=F32)
        nb = gate.shape[1]
        idx = lax.broadcasted_iota(jnp.int32, gate.shape, 1)
        lane = lax.broadcasted_iota(jnp.int32, (dec_seq, LANES), 1)
        out = jnp.zeros((dec_seq, LANES), jnp.int32)
        g = gate
        for r in range(MOBA_TOPK):
            m = jnp.max(g, axis=1, keepdims=True)
            first = jnp.min(jnp.where(g == m, idx, nb), axis=1, keepdims=True)
            out = jnp.where(lane == r, first, out)
            g = jnp.where(idx == first, -jnp.inf, g)
        idx_ref[h] = out


def _sample_select(q, kmean, dec_batch, dec_seq):
    nb = kmean.shape[1]
    return pl.pallas_call(
        functools.partial(_sample_select_body, dec_seq=dec_seq),
        grid=(dec_batch,),
        in_specs=[
            pl.BlockSpec((dec_seq, WIDTH), lambda b: (b, 0)),
            pl.BlockSpec((None, nb, N_HEADS, HEAD_DIM), lambda b: (b, 0, 0, 0)),
        ],
        out_specs=pl.BlockSpec((None, N_HEADS, dec_seq, LANES), lambda b: (b, 0, 0, 0)),
        out_shape=jax.ShapeDtypeStruct((dec_batch, N_HEADS, dec_seq, LANES), jnp.int32),
        compiler_params=_params("arbitrary"),
        name="moba_sample_select",
    )(q, kmean)


def _sample_attn_body(idx_ref, pt_ref, q_ref, kn_ref, vn_ref, ck_hbm, cv_hbm, o_ref, kbuf, vbuf, sem,
                      *, layer, dec_seq, ppb, page, n_pages):
    n_sel = MOBA_TOPK * ppb
    step = pl.program_id(0) * N_HEADS + pl.program_id(1)
    n_steps = pl.num_programs(0) * N_HEADS
    slot = step % 2

    def slab_copies(at_step, at_slot, qi, s):
        head = at_step % N_HEADS
        block = idx_ref[(at_step * dec_seq + qi) * MOBA_TOPK + s // ppb]
        phys = pt_ref[(at_step // N_HEADS) * n_pages + block * ppb + s % ppb]
        dst = pl.ds(s * page, page)
        return (pltpu.make_async_copy(ck_hbm.at[layer, phys, :, head, :], kbuf.at[at_slot, qi, dst, :], sem.at[0, at_slot]),
                pltpu.make_async_copy(cv_hbm.at[layer, phys, :, head, :], vbuf.at[at_slot, qi, dst, :], sem.at[1, at_slot]))

    def for_all_slabs(at_step, at_slot, action):
        for qi in range(dec_seq):
            for s in range(n_sel):
                for c in slab_copies(at_step, at_slot, qi, s):
                    action(c)

    @pl.when(step == 0)
    def _():
        for_all_slabs(step, slot, lambda c: c.start())

    @pl.when(step + 1 < n_steps)
    def _():
        for_all_slabs(step + 1, 1 - slot, lambda c: c.start())

    for_all_slabs(step, slot, lambda c: c.wait())

    qb = q_ref[...].astype(BF16)
    vn = vn_ref[...]
    s_own = lax.dot_general(kn_ref[...].astype(BF16), qb, _NT, preferred_element_type=F32)
    kpos = lax.broadcasted_iota(jnp.int32, s_own.shape, 0)
    qpos = lax.broadcasted_iota(jnp.int32, s_own.shape, 1)
    s_own = jnp.where(kpos <= qpos, s_own, -jnp.inf)
    rows = []
    for qi in range(dec_seq):
        s_all = lax.dot_general(kbuf[slot, qi].astype(BF16), qb, _NT, preferred_element_type=F32)
        s_sel = s_all[:, qi:qi + 1]
        s_new = s_own[:, qi:qi + 1]
        m = jnp.maximum(jnp.max(s_sel, axis=0, keepdims=True), jnp.max(s_new, axis=0, keepdims=True))
        p_sel = jnp.exp(s_sel - m)
        p_new = jnp.exp(s_new - m)
        l = jnp.sum(p_sel, axis=0, keepdims=True) + jnp.sum(p_new, axis=0, keepdims=True)
        pv = jnp.sum(p_sel * vbuf[slot, qi], axis=0, keepdims=True) + jnp.sum(p_new * vn, axis=0, keepdims=True)
        rows.append(pv / l)
    o_ref[...] = jnp.concatenate(rows, axis=0).astype(o_ref.dtype)


def _sample_attn(idx_flat, pt_flat, q, k_new, v_new, cache_k, cache_v, layer, dec_batch, dec_seq):
    page = cache_k.shape[2]
    ppb = MOBA_BLOCK // page
    n_keys = MOBA_TOPK * MOBA_BLOCK
    row = lambda b, h, idx, pt: (b, h)
    return pl.pallas_call(
        functools.partial(_sample_attn_body, layer=layer, dec_seq=dec_seq, ppb=ppb, page=page,
                          n_pages=pt_flat.shape[0] // dec_batch),
        grid_spec=pltpu.PrefetchScalarGridSpec(
            num_scalar_prefetch=2,
            grid=(dec_batch, N_HEADS),
            in_specs=[
                pl.BlockSpec((dec_seq, HEAD_DIM), row),
                pl.BlockSpec((dec_seq, HEAD_DIM), row),
                pl.BlockSpec((dec_seq, HEAD_DIM), row),
                pl.BlockSpec(memory_space=pl.ANY),
                pl.BlockSpec(memory_space=pl.ANY),
            ],
            out_specs=pl.BlockSpec((dec_seq, HEAD_DIM), row),
            scratch_shapes=[
                pltpu.VMEM((2, dec_seq, n_keys, HEAD_DIM), F32),
                pltpu.VMEM((2, dec_seq, n_keys, HEAD_DIM), F32),
                pltpu.SemaphoreType.DMA((2, 2)),
            ],
        ),
        out_shape=jax.ShapeDtypeStruct((dec_batch * dec_seq, WIDTH), F32),
        compiler_params=_params("arbitrary", "arbitrary"),
        name="moba_sample_attn",
    )(idx_flat, pt_flat, q, k_new, v_new, cache_k, cache_v)


def _moba_sample(q, k_new, v_new, cache_k, cache_v, page_table, layer, dec_batch, dec_seq, kmean=None):
    n_pages = page_table.shape[1]
    page = cache_k.shape[2]
    ppb = MOBA_BLOCK // page
    past = n_pages * page
    assert past % MOBA_BLOCK == 0 and dec_seq <= MOBA_BLOCK and past // MOBA_BLOCK >= MOBA_TOPK
    assert dec_seq % SUBLANES == 0
    pt_flat = page_table.reshape(-1)
    if kmean is None:
        kmean = _block_means(cache_k, pt_flat, layer, dec_batch, n_pages)
    idx = _sample_select(q, kmean, dec_batch, dec_seq)[..., :MOBA_TOPK]
    o = _sample_attn(idx.reshape(-1), pt_flat, q, k_new, v_new, cache_k, cache_v, layer, dec_batch, dec_seq)
    return o.astype(BF16)


def _deltanet_body(x_ref, z_ref, ba_ref, cbuf_ref, cw_ref, alog_ref, dtb_ref, nw_ref, s0_ref,
                   o_ref, sout_ref, xp_scr, s_scr, *, n_valid, rows):
    c = DN_CHUNK
    step = pl.program_id(1)

    @pl.when(step == 0)
    def _():
        s_scr[...] = s0_ref[...]
        xp_scr[:, 0:SUBLANES, :] = cbuf_ref[...]

    @pl.when(step > 0)
    def _():
        xp_scr[:, 0:SUBLANES, :] = xp_scr[:, c:c + SUBLANES, :]

    xp_scr[:, SUBLANES:, :] = x_ref[...]

    def l2n(a):
        return a * lax.rsqrt(jnp.sum(a * a, axis=-1, keepdims=True) + NORM_EPS)

    def mm(a, b, dims=None):
        a, b = a.astype(BF16), b.astype(BF16)
        if dims is None:
            return jnp.dot(a, b, preferred_element_type=F32)
        return lax.dot_general(a, b, dims, preferred_element_type=F32)

    ri = lax.broadcasted_iota(jnp.int32, (c, c), 0)
    ci = lax.broadcasted_iota(jnp.int32, (c, c), 1)
    causal = ri >= ci
    strict = ri > ci
    eye = (ri == ci).astype(F32)
    tril_ones = causal.astype(F32)
    first = SUBLANES - (CONV_W - 1)

    for r in range(rows):
        def conv(col):
            lanes = slice(col, col + HEAD_DIM)
            y = xp_scr[r, first:first + c, lanes] * cw_ref[0:1, lanes]
            for j in range(1, CONV_W):
                y = y + xp_scr[r, first + j:first + j + c, lanes] * cw_ref[j:j + 1, lanes]
            return _silu(y)

        ba = ba_ref[r]
        beta = jax.nn.sigmoid(ba)
        xs = ba + dtb_ref[...]
        g = -jnp.exp(alog_ref[...]) * (jnp.maximum(xs, 0.0) + jnp.log1p(jnp.exp(-jnp.abs(xs))))
        if n_valid < c:
            live = lax.broadcasted_iota(jnp.int32, ba.shape, 0) < n_valid
            beta = jnp.where(live, beta, 0.0)
            g = jnp.where(live, g, 0.0)
        gc = jnp.dot(tril_ones, g, precision=HIGHEST, preferred_element_type=F32)
        gct = gc.T
        eg = jnp.exp(gc)
        g_last = gc[c - 1:c, :]
        kdf = jnp.exp(g_last - gc)
        g_tot = jnp.exp(g_last)

        hs = range(N_HEADS)
        col = lambda a, h: a[:, N_HEADS + h:N_HEADS + h + 1]
        qn = [l2n(conv(h * HEAD_DIM)) * (HEAD_DIM ** -0.5) for h in hs]
        kn = [l2n(conv(WIDTH + h * HEAD_DIM)) for h in hs]
        kb = [kn[h] * beta[:, h:h + 1] for h in hs]
        kk_qk = [mm(jnp.concatenate([kb[h], qn[h]], axis=0), kn[h], _NT) for h in hs]
        decay = [jnp.where(causal, jnp.exp(jnp.where(causal, col(gc, h) - gct[N_HEADS + h:N_HEADS + h + 1, :], 0.0)), 0.0)
                 for h in hs]
        a_qk = [jnp.where(causal, kk_qk[h][c:] * decay[h], 0.0) for h in hs]
        pw = [jnp.where(strict, -kk_qk[h][:c] * decay[h], 0.0) for h in hs]
        t_inv = [eye + pw[h] for h in hs]
        pw = [mm(pw[h], pw[h]) for h in hs]
        for _ in range(int(math.log2(c)) - 2):
            both = [mm(jnp.concatenate([t_inv[h], pw[h]], axis=0), pw[h]) for h in hs]
            t_inv = [t_inv[h] + both[h][:c] for h in hs]
            pw = [both[h][c:] for h in hs]
        t_inv = [t_inv[h] + mm(t_inv[h], pw[h]) for h in hs]
        vb_kbe = [jnp.concatenate([conv(2 * WIDTH + h * HEAD_DIM) * beta[:, h:h + 1], kb[h] * col(eg, h)], axis=1)
                  for h in hs]
        uw = [mm(t_inv[h], vb_kbe[h]) for h in hs]
        s_old = [s_scr[r, h] for h in hs]
        sq = [mm(jnp.concatenate([uw[h][:, HEAD_DIM:], qn[h] * col(eg, h)], axis=0), s_old[h]) for h in hs]
        v_new = [uw[h][:, :HEAD_DIM] - sq[h][:c] for h in hs]
        o = [sq[h][c:] + mm(a_qk[h], v_new[h]) for h in hs]
        kv = [mm(kn[h] * col(kdf, h), v_new[h], _TN) for h in hs]
        for h in hs:
            s_scr[r, h] = s_old[h] * col(g_tot, h) + kv[h]
            lanes = slice(h * HEAD_DIM, (h + 1) * HEAD_DIM)
            o_ref[r, :, lanes] = (_rms(o[h], nw_ref[...]) * _silu(z_ref[r, :, lanes])).astype(o_ref.dtype)

    @pl.when(step == pl.num_programs(1) - 1)
    def _():
        sout_ref[...] = s_scr[...]


def _deltanet(qkv, zba, conv_buf, state0, conv_w, a_log, dt_bias, norm_w, batch, seq, n_valid, rows):
    c = DN_CHUNK
    assert seq % c == 0 and (n_valid == c or seq == c) and batch % rows == 0
    n_chunks = seq // c
    cw3 = 3 * WIDTH
    cbuf = jnp.pad(conv_buf, ((0, 0), (SUBLANES - (CONV_W - 1), 0), (0, 0)))
    pad = (0, LANES - 2 * N_HEADS)
    alog = jnp.pad(jnp.concatenate([jnp.zeros_like(a_log), a_log]), pad).reshape(1, LANES)
    dtb = jnp.pad(jnp.concatenate([jnp.zeros_like(dt_bias), dt_bias]), pad).reshape(1, LANES)
    zba3 = zba.reshape(batch, seq, WIDTH + LANES)
    tok = lambda b, i: (b, i, 0)
    per_seq3 = lambda b, i: (b, 0, 0)
    per_seq4 = lambda b, i: (b, 0, 0, 0)
    fixed = lambda b, i: (0, 0)
    o, s_out = pl.pallas_call(
        functools.partial(_deltanet_body, n_valid=n_valid, rows=rows),
        grid=(batch // rows, n_chunks),
        in_specs=[
            pl.BlockSpec((rows, c, cw3), tok),
            pl.BlockSpec((rows, c, WIDTH), tok),
            pl.BlockSpec((rows, c, LANES), lambda b, i: (b, i, WIDTH // LANES)),
            pl.BlockSpec((rows, SUBLANES, cw3), per_seq3),
            pl.BlockSpec((CONV_W, cw3), fixed),
            pl.BlockSpec((1, LANES), fixed),
            pl.BlockSpec((1, LANES), fixed),
            pl.BlockSpec((1, HEAD_DIM), fixed),
            pl.BlockSpec((rows, N_HEADS, HEAD_DIM, HEAD_DIM), per_seq4),
        ],
        out_specs=[
            pl.BlockSpec((rows, c, WIDTH), tok),
            pl.BlockSpec((rows, N_HEADS, HEAD_DIM, HEAD_DIM), per_seq4),
        ],
        out_shape=[
            jax.ShapeDtypeStruct((batch, seq, WIDTH), BF16),
            jax.ShapeDtypeStruct((batch, N_HEADS, HEAD_DIM, HEAD_DIM), F32),
        ],
        scratch_shapes=[
            pltpu.VMEM((rows, c + SUBLANES, cw3), F32),
            pltpu.VMEM((rows, N_HEADS, HEAD_DIM, HEAD_DIM), F32),
        ],
        compiler_params=_params("arbitrary", "arbitrary"),
        name="deltanet",
    )(qkv.reshape(batch, seq, cw3), zba3, zba3, cbuf, conv_w, alog, dtb, norm_w.reshape(1, HEAD_DIM), state0)
    return o.reshape(batch * seq, WIDTH), s_out


def _out_proj_body(oa_ref, od_ref, x_ref, w_ref, wpost_ref, wpre_ref, x1_ref, h2_ref):
    mix = (jnp.dot(oa_ref[...], w_ref[0:WIDTH, :], preferred_element_type=F32)
           + jnp.dot(od_ref[...], w_ref[WIDTH:, :], preferred_element_type=F32))
    x1 = x_ref[...] + _rms(mix, wpost_ref[...])
    x1_ref[...] = x1
    h2_ref[...] = _rms(x1, wpre_ref[...]).astype(h2_ref.dtype)


def _out_proj(o_a, o_d, x, w_out, w_post, w_pre):
    t, d = x.shape
    tm = _row_tile(t, 512)
    row = lambda i: (i, 0)
    fixed = lambda i: (0, 0)
    return pl.pallas_call(
        _out_proj_body,
        grid=(t // tm,),
        in_specs=[
            pl.BlockSpec((tm, WIDTH), row),
            pl.BlockSpec((tm, WIDTH), row),
            pl.BlockSpec((tm, d), row),
            pl.BlockSpec((2 * WIDTH, d), fixed),
            pl.BlockSpec((1, d), fixed),
            pl.BlockSpec((1, d), fixed),
        ],
        out_specs=[pl.BlockSpec((tm, d), row), pl.BlockSpec((tm, d), row)],
        out_shape=[jax.ShapeDtypeStruct((t, d), F32), jax.ShapeDtypeStruct((t, d), BF16)],
        compiler_params=_params("arbitrary"),
        name="out_proj",
    )(o_a, o_d, x, w_out, w_post, w_pre)


def _mlp_body(pt_ref, h_ref, x1_ref, wu_ref, wd_ref, wpost_ref, *rest, n_pages, ppb):
    del pt_ref
    y_ref = rest[n_pages]
    f = pl.program_id(1)

    @pl.when(f == 0)
    def _():
        y_ref[...] = jnp.zeros_like(y_ref)

    a = jnp.dot(h_ref[...], wu_ref[...], preferred_element_type=F32)
    a = jnp.square(jnp.maximum(a, 0.0)).astype(BF16)
    y_ref[...] += jnp.dot(a, wd_ref[...], preferred_element_type=F32)
    if n_pages:
        _sum_pages(rest[:n_pages], rest[n_pages + 1], ppb)

    @pl.when(f == pl.num_programs(1) - 1)
    def _():
        y_ref[...] = x1_ref[...] + _rms(y_ref[...], wpost_ref[...])


def _mlp_tiles(t, d_ff):
    tm, tf = _row_tile(t, 512), min(1024, d_ff)
    assert d_ff % tf == 0
    return tm, tf


def _mlp_pages_per_step(t, d_ff, total_pages, ppb):
    tm, tf = _mlp_tiles(t, d_ff)
    n_steps = (t // tm) * (d_ff // tf)
    per_step = total_pages // n_steps
    ok = per_step * n_steps == total_pages and per_step % ppb == 0 and 0 < per_step <= 2 * SUBLANES
    return per_step if ok else 0


def _mlp(h2, x1, w_up, w_down, w_post, page_job=None):
    t, d = x1.shape
    d_ff = w_up.shape[1]
    tm, tf = _mlp_tiles(t, d_ff)
    nf = d_ff // tf
    n_steps = (t // tm) * nf
    in_specs = [
        pl.BlockSpec((tm, d), lambda i, f, pt: (i, 0)),
        pl.BlockSpec((tm, d), lambda i, f, pt: (i, 0)),
        pl.BlockSpec((d, tf), lambda i, f, pt: (0, f)),
        pl.BlockSpec((tf, d), lambda i, f, pt: (f, 0)),
        pl.BlockSpec((1, d), lambda i, f, pt: (0, 0)),
    ]
    out_specs = [pl.BlockSpec((tm, d), lambda i, f, pt: (i, 0))]
    out_shape = [jax.ShapeDtypeStruct((t, d), F32)]
    args = [h2, x1, w_up, w_down, w_post]
    n_pages, ppb, pt = 0, 1, jnp.zeros((1,), jnp.int32)
    if page_job is not None:
        cache_k, pt, layer = page_job
        page = cache_k.shape[2]
        ppb = MOBA_BLOCK // page
        n_pages = _mlp_pages_per_step(t, d_ff, pt.shape[0], ppb)
        assert n_pages > 0
        in_specs += [pl.BlockSpec((None, None, page, N_HEADS, HEAD_DIM),
                                  lambda i, f, pt, r=r: (layer, pt[(i * nf + f) * n_pages + r], 0, 0, 0))
                     for r in range(n_pages)]
        args += [cache_k] * n_pages
        out_specs.append(pl.BlockSpec((None, n_pages // ppb, N_HEADS, HEAD_DIM), lambda i, f, pt: (i * nf + f, 0, 0, 0)))
        out_shape.append(jax.ShapeDtypeStruct((n_steps, n_pages // ppb, N_HEADS, HEAD_DIM), F32))
    res = pl.pallas_call(
        functools.partial(_mlp_body, n_pages=n_pages, ppb=ppb),
        grid_spec=pltpu.PrefetchScalarGridSpec(
            num_scalar_prefetch=1,
            grid=(t // tm, nf),
            in_specs=in_specs,
            out_specs=out_specs,
        ),
        out_shape=out_shape,
        compiler_params=_params("arbitrary", "arbitrary"),
        name="mlp",
    )(pt, *args)
    return res if page_job is not None else res[0]


def _rope_tables(pos, rows, scale):
    half = HEAD_DIM // 2
    inv_freq = 1.0 / (ROPE_THETA ** (jnp.arange(half, dtype=F32) / half))
    ang = pos.astype(F32)[:, None] * inv_freq[None, :]
    cos = jnp.concatenate([jnp.cos(ang), jnp.cos(ang)], axis=1) * scale
    sin = jnp.concatenate([-jnp.sin(ang), jnp.sin(ang)], axis=1) * scale
    reps = max(1, rows // pos.shape[0])
    return jnp.tile(cos, (reps, 1)), jnp.tile(sin, (reps, 1))


def _layer(x, pos, conv_buf, dn_state, moba_fn, lw, *, q_dtype, page_job=None):
    (w_pre_mix, w_q, w_k, w_v, w_d, w_zba, conv_w, a_log, dt_bias, dn_norm_w, w_out, w_post_mix, w_pre_mlp,
     w_up, w_down, w_post_mlp) = lw
    batch, seq, d = x.shape
    t = batch * seq
    x2 = x.reshape(t, d)
    tm = _row_tile(t, 1024)
    assert tm % seq == 0 or seq % tm == 0
    h = _norm_cast(x2, w_pre_mix.reshape(1, d))
    q = _proj(h, w_q, out_dtype=q_dtype, rope=_rope_tables(pos, tm, ATTN_SCALE))
    k = _proj(h, w_k, out_dtype=F32, rope=_rope_tables(pos, tm, 1.0))
    v = _proj(h, w_v, out_dtype=F32)
    qkv_d = _proj(h, w_d, out_dtype=F32)
    zba = _proj(h, w_zba, out_dtype=F32)
    o_a = moba_fn(q, k, v)

    c = DN_CHUNK
    seq_pad = -(-seq // c) * c
    if seq_pad == seq:
        dn_in, n_valid = (qkv_d, zba), c
    else:
        assert seq < c
        padrows = lambda a: jnp.pad(a.reshape(batch, seq, -1), ((0, 0), (0, c - seq), (0, 0))).reshape(batch * c, -1)
        dn_in, n_valid = (padrows(qkv_d), padrows(zba)), seq
    o_d, new_state = _deltanet(*dn_in, conv_buf, dn_state, conv_w, a_log, dt_bias, dn_norm_w, batch, seq_pad, n_valid,
                               rows=DN_ROWS)
    if seq_pad != seq:
        o_d = o_d.reshape(batch, c, WIDTH)[:, :seq].reshape(t, WIDTH)
    keep = CONV_W - 1
    tail = qkv_d.reshape(batch, seq, -1)[:, max(0, seq - keep):]
    new_buf = tail if seq >= keep else jnp.concatenate([conv_buf, tail], axis=1)[:, -keep:]

    x1, h2 = _out_proj(o_a, o_d, x2, w_out, w_post_mix.reshape(1, d), w_pre_mlp.reshape(1, d))
    kmean = None
    if page_job is not None and _mlp_pages_per_step(t, w_up.shape[1], page_job[1].shape[0],
                                                    MOBA_BLOCK // page_job[0].shape[2]):
        y, kmean = _mlp(h2, x1, w_up, w_down, w_post_mlp.reshape(1, d), page_job)
    else:
        y = _mlp(h2, x1, w_up, w_down, w_post_mlp.reshape(1, d))
    heads = (batch, seq, N_HEADS, HEAD_DIM)
    return y.reshape(batch, seq, d), k.reshape(heads), v.reshape(heads), new_buf, new_state, kmean


def kernel(x_prompt, x_sample, cache_k, cache_v, state_dn, state_conv, page_table, w_pre_mix, w_in, conv_w,
           a_log, dt_bias, dn_norm_w, w_out, w_post_mix, w_pre_mlp, w_up, w_down, w_post_mlp):
    depth = w_in.shape[0]
    batch, seq, _ = x_prompt.shape
    dec_batch, dec_seq, _ = x_sample.shape
    past = page_table.shape[1] * cache_k.shape[2]
    pos_p = jnp.arange(seq, dtype=jnp.int32)
    pos_s = past + jnp.arange(dec_seq, dtype=jnp.int32)
    cuts = [0, WIDTH, 2 * WIDTH, 3 * WIDTH, 6 * WIDTH, 7 * WIDTH]
    hp, hs = x_prompt, x_sample
    outs = [[] for _ in range(8)]
    for l in range(depth):
        w_slices = [w_in[l][:, a:b].astype(BF16) for a, b in zip(cuts[:-1], cuts[1:])]
        w_ba = jnp.pad(w_in[l][:, cuts[-1]:], ((0, 0), (0, LANES - 2 * N_HEADS))).astype(BF16)
        w_slices[-1] = jnp.concatenate([w_slices[-1], w_ba], axis=1)
        lw = (w_pre_mix[l], *w_slices, conv_w[l], a_log[l], dt_bias[l], dn_norm_w[l],
              w_out[l].astype(BF16), w_post_mix[l], w_pre_mlp[l], w_up[l].astype(BF16), w_down[l].astype(BF16),
              w_post_mlp[l])
        zero_buf = jnp.zeros((batch, CONV_W - 1, 3 * WIDTH), x_prompt.dtype)
        zero_state = jnp.zeros((batch, N_HEADS, HEAD_DIM, HEAD_DIM), state_dn.dtype)
        prompt_moba = functools.partial(_moba_prompt, batch=batch, seq=seq)
        hp, kp, vp, cvp, dnp, kmean = _layer(hp, pos_p, zero_buf, zero_state, prompt_moba, lw, q_dtype=BF16,
                                             page_job=(cache_k, page_table.reshape(-1), l))
        if kmean is not None:
            kmean = kmean.reshape(dec_batch, -1, N_HEADS, HEAD_DIM)
        sample_moba = functools.partial(_moba_sample, cache_k=cache_k, cache_v=cache_v, page_table=page_table,
                                        layer=l, dec_batch=dec_batch, dec_seq=dec_seq, kmean=kmean)
        hs, kss, vss, cvs, dns, _ = _layer(hs, pos_s, state_conv[l], state_dn[l], sample_moba, lw, q_dtype=F32)
        for lst, val in zip(outs, (kp, vp, kss, vss, dnp, dns, cvp, cvs)):
            lst.append(val)
    return (hp, hs) + tuple(jnp.stack(o) for o in outs)
```

```python
import functools
import math

import jax
import jax.numpy as jnp
from jax import lax
from jax.experimental import pallas as pl
from jax.experimental.pallas import tpu as pltpu

F32 = jnp.float32
BF16 = jnp.bfloat16
HIGHEST = lax.Precision.HIGHEST

HEAD_DIM = 128
N_HEADS = 8
WIDTH = N_HEADS * HEAD_DIM
MOBA_BLOCK = 256
MOBA_TOPK = 3
DN_CHUNK = 64
DN_ROWS = 2
CONV_W = 4
ROPE_THETA = 10000.0
NORM_EPS = 1e-6
ATTN_SCALE = HEAD_DIM ** -0.5
LANES = 128
SUBLANES = 8
MXU_WIDTH = 256
VMEM_LIMIT = 60 * 1024 * 1024

_NT = (((1,), (1,)), ((), ()))
_TN = (((0,), (0,)), ((), ()))


def _params(*sem):
    return pltpu.CompilerParams(dimension_semantics=sem, vmem_limit_bytes=VMEM_LIMIT)


def _rms(x, w):
    return x * lax.rsqrt(jnp.mean(x * x, axis=-1, keepdims=True) + NORM_EPS) * w


def _silu(x):
    return x * jax.nn.sigmoid(x)


def _row_tile(t, cap):
    tm = min(t, cap)
    assert t % tm == 0
    return tm


def _proj_body(h_ref, w_ref, *rest, rope, tn, fused_norm):
    if fused_norm:
        wn_ref, rest, hn_ref = rest[0], rest[1:-1], rest[-1]
        h = _rms(h_ref[...], wn_ref[...]).astype(BF16)
        hn_ref[...] = h
    else:
        h = h_ref[...]
    o_ref = rest[-1]
    for c0 in range(0, tn, MXU_WIDTH):
        chunk = min(MXU_WIDTH, tn - c0)
        acc = jnp.dot(h, w_ref[:, c0:c0 + chunk], preferred_element_type=F32)
        if rope:
            cos, sin = rest[0][...], rest[1][...]
            acc = jnp.concatenate(
                [acc[:, c:c + HEAD_DIM] * cos + pltpu.roll(acc[:, c:c + HEAD_DIM], HEAD_DIM // 2, 1) * sin
                 for c in range(0, chunk, HEAD_DIM)], axis=1)
        o_ref[:, c0:c0 + chunk] = acc.astype(o_ref.dtype)


def _proj(h, w, *, out_dtype, rope=None, norm_w=None):
    t, d = h.shape
    n = w.shape[1]
    tm = _row_tile(t, 1024)
    tn = n if n <= 1536 else 1024
    assert n % tn == 0 and tn % LANES == 0
    fused_norm = norm_w is not None
    assert not fused_norm or n == tn
    in_specs = [pl.BlockSpec((tm, d), lambda i, j: (i, 0)), pl.BlockSpec((d, tn), lambda i, j: (0, j))]
    args = [h, w]
    if fused_norm:
        in_specs.append(pl.BlockSpec((1, d), lambda i, j: (0, 0)))
        args.append(norm_w)
    if rope is not None:
        n_rope = rope[0].shape[0] // tm
        in_specs += [pl.BlockSpec((tm, HEAD_DIM), lambda i, j: (i % n_rope, 0))] * 2
        args += list(rope)
    out_specs = [pl.BlockSpec((tm, tn), lambda i, j: (i, j))]
    out_shape = [jax.ShapeDtypeStruct((t, n), out_dtype)]
    if fused_norm:
        out_specs.append(pl.BlockSpec((tm, d), lambda i, j: (i, 0)))
        out_shape.append(jax.ShapeDtypeStruct((t, d), BF16))
    res = pl.pallas_call(
        functools.partial(_proj_body, rope=rope is not None, tn=tn, fused_norm=fused_norm),
        grid=(t // tm, n // tn),
        in_specs=in_specs,
        out_specs=out_specs,
        out_shape=out_shape,
        compiler_params=_params("arbitrary", "arbitrary"),
        name=("norm_" if fused_norm else "") + ("proj_rope" if rope is not None else "proj"),
    )(*args)
    return res if fused_norm else res[0]


def _topk_mask(gate, valid, axis):
    n = gate.shape[axis]
    idx = lax.broadcasted_iota(jnp.int32, gate.shape, axis)
    g = jnp.where(valid, gate, -jnp.inf)
    sel = jnp.zeros(gate.shape, jnp.bool_)
    for _ in range(min(MOBA_TOPK, n)):
        m = jnp.max(g, axis=axis, keepdims=True)
        first = jnp.min(jnp.where(g == m, idx, n), axis=axis, keepdims=True)
        hit = idx == first
        sel = sel | hit
        g = jnp.where(hit, -jnp.inf, g)
    return sel & valid


def _moba_prompt_body(q_ref, k_ref, v_ref, o_ref, kb_scr, vt_scr, km_scr, sel_scr, m_scr, l_scr, acc_scr, *, nb):
    blk = MOBA_BLOCK
    qi = pl.program_id(1)
    heads = [slice(h * HEAD_DIM, (h + 1) * HEAD_DIM) for h in range(N_HEADS)]

    @pl.when(qi == 0)
    def _():
        for h in range(N_HEADS):
            for j in range(nb):
                rows = slice(j * blk, (j + 1) * blk)
                k = k_ref[rows, heads[h]]
                kb_scr[h, rows, :] = k.astype(BF16)
                vt_scr[h, :, rows] = v_ref[rows, heads[h]].T.astype(BF16)
                km_scr[h, j:j + 1, :] = jnp.sum(k, axis=0, keepdims=True) / blk

    own = pl.multiple_of(qi * blk, blk)
    kpos = lax.broadcasted_iota(jnp.int32, (blk, blk), 0)
    qpos = lax.broadcasted_iota(jnp.int32, (blk, blk), 1)
    block_id = lax.broadcasted_iota(jnp.int32, (nb, blk), 0)
    hs = range(N_HEADS)
    gate = [lax.dot_general(km_scr[h], q_ref[:, heads[h]].astype(F32), _NT, precision=HIGHEST,
                            preferred_element_type=F32) for h in hs]
    s = [lax.dot_general(kb_scr[h, pl.ds(own, blk), :], q_ref[:, heads[h]], _NT, preferred_element_type=F32)
         for h in hs]
    p = []
    for h in hs:
        sel_scr[h] = _topk_mask(gate[h], block_id < qi, 0).astype(F32)
        s_h = jnp.where(kpos <= qpos, s[h], -jnp.inf)
        m = jnp.max(s_h, axis=0, keepdims=True)
        p_h = jnp.exp(s_h - m)
        m_scr[h] = m
        l_scr[h] = jnp.sum(p_h, axis=0, keepdims=True)
        p.append(p_h.astype(BF16))
    for h in hs:
        acc_scr[h] = jnp.dot(vt_scr[h, :, pl.ds(own, blk)], p[h], preferred_element_type=F32)

    def past_block(j, carry):
        off = pl.multiple_of(j * blk, blk)

        s = [lax.dot_general(kb_scr[h, pl.ds(off, blk), :], q_ref[:, heads[h]], _NT, preferred_element_type=F32)
             for h in hs]
        p, alpha = [], []
        for h in hs:
            s_h = jnp.where(sel_scr[h, pl.ds(j, 1), :] > 0, s[h], -jnp.inf)
            m_old = m_scr[h]
            m_new = jnp.maximum(m_old, jnp.max(s_h, axis=0, keepdims=True))
            a_h = jnp.exp(m_old - m_new)
            p_h = jnp.exp(s_h - m_new)
            m_scr[h] = m_new
            l_scr[h] = a_h * l_scr[h] + jnp.sum(p_h, axis=0, keepdims=True)
            p.append(p_h.astype(BF16))
            alpha.append(a_h)
        pv = [jnp.dot(vt_scr[h, :, pl.ds(off, blk)], p[h], preferred_element_type=F32) for h in hs]
        for h in hs:
            acc_scr[h] = alpha[h] * acc_scr[h] + pv[h]
        return carry

    lax.fori_loop(0, qi, past_block, 0)
    for h in range(N_HEADS):
        o_ref[:, heads[h]] = (acc_scr[h] / l_scr[h]).T.astype(o_ref.dtype)


def _moba_prompt(q, k, v, batch, seq):
    assert seq % MOBA_BLOCK == 0
    nb = seq // MOBA_BLOCK
    t = batch * seq
    return pl.pallas_call(
        functools.partial(_moba_prompt_body, nb=nb),
        grid=(batch, nb),
        in_specs=[
            pl.BlockSpec((MOBA_BLOCK, WIDTH), lambda b, i: (b * nb + i, 0)),
            pl.BlockSpec((seq, WIDTH), lambda b, i: (b, 0)),
            pl.BlockSpec((seq, WIDTH), lambda b, i: (b, 0)),
        ],
        out_specs=pl.BlockSpec((MOBA_BLOCK, WIDTH), lambda b, i: (b * nb + i, 0)),
        out_shape=jax.ShapeDtypeStruct((t, WIDTH), BF16),
        scratch_shapes=[
            pltpu.VMEM((N_HEADS, seq, HEAD_DIM), BF16),
            pltpu.VMEM((N_HEADS, HEAD_DIM, seq), BF16),
            pltpu.VMEM((N_HEADS, nb, HEAD_DIM), F32),
            pltpu.VMEM((N_HEADS, nb, MOBA_BLOCK), F32),
            pltpu.VMEM((N_HEADS, 1, MOBA_BLOCK), F32),
            pltpu.VMEM((N_HEADS, 1, MOBA_BLOCK), F32),
            pltpu.VMEM((N_HEADS, HEAD_DIM, MOBA_BLOCK), F32),
        ],
        compiler_params=_params("arbitrary", "arbitrary"),
        name="moba_prompt",
    )(q, k, v)


def _sum_pages(page_refs, out_ref, ppb):
    for i in range(len(page_refs) // ppb):
        tot = jnp.sum(page_refs[i * ppb][...], axis=0)
        for r in range(1, ppb):
            tot = tot + jnp.sum(page_refs[i * ppb + r][...], axis=0)
        out_ref[i] = tot / MOBA_BLOCK


def _page_sum_body(pt_ref, *refs, n_in, ppb):
    del pt_ref
    _sum_pages(refs[:n_in], refs[n_in], ppb)


def _block_means(cache_k, page_table_flat, layer, dec_batch, n_pages):
    page = cache_k.shape[2]
    ppb = MOBA_BLOCK // page
    assert n_pages % ppb == 0
    per_step = ppb * SUBLANES
    while n_pages % per_step:
        per_step //= 2
    assert per_step % ppb == 0
    steps = n_pages // per_step
    nb = n_pages // ppb

    def spec(r):
        return pl.BlockSpec((None, None, page, N_HEADS, HEAD_DIM),
                            lambda b, p, pt, r=r: (layer, pt[b * n_pages + p * per_step + r], 0, 0, 0))

    return pl.pallas_call(
        functools.partial(_page_sum_body, n_in=per_step, ppb=ppb),
        grid_spec=pltpu.PrefetchScalarGridSpec(
            num_scalar_prefetch=1,
            grid=(dec_batch, steps),
            in_specs=[spec(r) for r in range(per_step)],
            out_specs=pl.BlockSpec((None, per_step // ppb, N_HEADS, HEAD_DIM), lambda b, p, pt: (b, p, 0, 0)),
        ),
        out_shape=jax.ShapeDtypeStruct((dec_batch, nb, N_HEADS, HEAD_DIM), F32),
        compiler_params=_params("arbitrary", "arbitrary"),
        name="moba_block_means",
    )(page_table_flat, *([cache_k] * per_step))


def _sample_select_body(q_ref, km_ref, idx_ref, *, dec_seq):
    for h in range(N_HEADS):
        q = q_ref[:, h * HEAD_DIM:(h + 1) * HEAD_DIM]
        gate = lax.dot_general(q, km_ref[:, h, :], _NT, precision=HIGHEST, preferred_element_type=F32)
        nb = gate.shape[1]
        idx = lax.broadcasted_iota(jnp.int32, gate.shape, 1)
        lane = lax.broadcasted_iota(jnp.int32, (dec_seq, LANES), 1)
        out = jnp.zeros((dec_seq, LANES), jnp.int32)
        g = gate
        for r in range(MOBA_TOPK):
            m = jnp.max(g, axis=1, keepdims=True)
            first = jnp.min(jnp.where(g == m, idx, nb), axis=1, keepdims=True)
            out = jnp.where(lane == r, first, out)
            g = jnp.where(idx == first, -jnp.inf, g)
        idx_ref[h] = out


def _sample_select(q, kmean, dec_batch, dec_seq):
    nb = kmean.shape[1]
    return pl.pallas_call(
        functools.partial(_sample_select_body, dec_seq=dec_seq),
        grid=(dec_batch,),
        in_specs=[
            pl.BlockSpec((dec_seq, WIDTH), lambda b: (b, 0)),
            pl.BlockSpec((None, nb, N_HEADS, HEAD_DIM), lambda b: (b, 0, 0, 0)),
        ],
        out_specs=pl.BlockSpec((None, N_HEADS, dec_seq, LANES), lambda b: (b, 0, 0, 0)),
        out_shape=jax.ShapeDtypeStruct((dec_batch, N_HEADS, dec_seq, LANES), jnp.int32),
        compiler_params=_params("arbitrary"),
        name="moba_sample_select",
    )(q, kmean)


def _sample_attn_body(idx_ref, pt_ref, q_ref, kn_ref, vn_ref, ck_hbm, cv_hbm, o_ref, kbuf, vbuf, sem,
                      *, layer, dec_seq, ppb, page, n_pages):
    n_sel = MOBA_TOPK * ppb
    step = pl.program_id(0) * N_HEADS + pl.program_id(1)
    n_steps = pl.num_programs(0) * N_HEADS
    slot = step % 2

    def slab_copies(at_step, at_slot, qi, s):
        head = at_step % N_HEADS
        block = idx_ref[(at_step * dec_seq + qi) * MOBA_TOPK + s // ppb]
        phys = pt_ref[(at_step // N_HEADS) * n_pages + block * ppb + s % ppb]
        dst = pl.ds(s * page, page)
        return (pltpu.make_async_copy(ck_hbm.at[layer, phys, :, head, :], kbuf.at[at_slot, qi, dst, :], sem.at[0, at_slot]),
                pltpu.make_async_copy(cv_hbm.at[layer, phys, :, head, :], vbuf.at[at_slot, qi, dst, :], sem.at[1, at_slot]))

    def for_all_slabs(at_step, at_slot, action):
        for qi in range(dec_seq):
            for s in range(n_sel):
                for c in slab_copies(at_step, at_slot, qi, s):
                    action(c)

    @pl.when(step == 0)
    def _():
        for_all_slabs(step, slot, lambda c: c.start())

    @pl.when(step + 1 < n_steps)
    def _():
        for_all_slabs(step + 1, 1 - slot, lambda c: c.start())

    for_all_slabs(step, slot, lambda c: c.wait())

    qb = q_ref[...].astype(BF16)
    vn = vn_ref[...]
    s_own = lax.dot_general(kn_ref[...].astype(BF16), qb, _NT, preferred_element_type=F32)
    kpos = lax.broadcasted_iota(jnp.int32, s_own.shape, 0)
    qpos = lax.broadcasted_iota(jnp.int32, s_own.shape, 1)
    s_own = jnp.where(kpos <= qpos, s_own, -jnp.inf)
    rows = []
    for qi in range(dec_seq):
        s_all = lax.dot_general(kbuf[slot, qi].astype(BF16), qb, _NT, preferred_element_type=F32)
        s_sel = s_all[:, qi:qi + 1]
        s_new = s_own[:, qi:qi + 1]
        m = jnp.maximum(jnp.max(s_sel, axis=0, keepdims=True), jnp.max(s_new, axis=0, keepdims=True))
        p_sel = jnp.exp(s_sel - m)
        p_new = jnp.exp(s_new - m)
        l = jnp.sum(p_sel, axis=0, keepdims=True) + jnp.sum(p_new, axis=0, keepdims=True)
        pv = jnp.sum(p_sel * vbuf[slot, qi], axis=0, keepdims=True) + jnp.sum(p_new * vn, axis=0, keepdims=True)
        rows.append(pv / l)
    o_ref[...] = jnp.concatenate(rows, axis=0).astype(o_ref.dtype)


def _sample_attn(idx_flat, pt_flat, q, k_new, v_new, cache_k, cache_v, layer, dec_batch, dec_seq):
    page = cache_k.shape[2]
    ppb = MOBA_BLOCK // page
    n_keys = MOBA_TOPK * MOBA_BLOCK
    row = lambda b, h, idx, pt: (b, h)
    return pl.pallas_call(
        functools.partial(_sample_attn_body, layer=layer, dec_seq=dec_seq, ppb=ppb, page=page,
                          n_pages=pt_flat.shape[0] // dec_batch),
        grid_spec=pltpu.PrefetchScalarGridSpec(
            num_scalar_prefetch=2,
            grid=(dec_batch, N_HEADS),
            in_specs=[
                pl.BlockSpec((dec_seq, HEAD_DIM), row),
                pl.BlockSpec((dec_seq, HEAD_DIM), row),
                pl.BlockSpec((dec_seq, HEAD_DIM), row),
                pl.BlockSpec(memory_space=pl.ANY),
                pl.BlockSpec(memory_space=pl.ANY),
            ],
            out_specs=pl.BlockSpec((dec_seq, HEAD_DIM), row),
            scratch_shapes=[
                pltpu.VMEM((2, dec_seq, n_keys, HEAD_DIM), F32),
                pltpu.VMEM((2, dec_seq, n_keys, HEAD_DIM), F32),
                pltpu.SemaphoreType.DMA((2, 2)),
            ],
        ),
        out_shape=jax.ShapeDtypeStruct((dec_batch * dec_seq, WIDTH), F32),
        compiler_params=_params("arbitrary", "arbitrary"),
        name="moba_sample_attn",
    )(idx_flat, pt_flat, q, k_new, v_new, cache_k, cache_v)


def _moba_sample(q, k_new, v_new, cache_k, cache_v, page_table, layer, dec_batch, dec_seq, kmean=None):
    n_pages = page_table.shape[1]
    page = cache_k.shape[2]
    ppb = MOBA_BLOCK // page
    past = n_pages * page
    assert past % MOBA_BLOCK == 0 and dec_seq <= MOBA_BLOCK and past // MOBA_BLOCK >= MOBA_TOPK
    assert dec_seq % SUBLANES == 0
    pt_flat = page_table.reshape(-1)
    if kmean is None:
        kmean = _block_means(cache_k, pt_flat, layer, dec_batch, n_pages)
    idx = _sample_select(q, kmean, dec_batch, dec_seq)[..., :MOBA_TOPK]
    o = _sample_attn(idx.reshape(-1), pt_flat, q, k_new, v_new, cache_k, cache_v, layer, dec_batch, dec_seq)
    return o.astype(BF16)


def _deltanet_body(x_ref, z_ref, ba_ref, cbuf_ref, cw_ref, alog_ref, dtb_ref, nw_ref, s0_ref,
                   o_ref, sout_ref, xp_scr, s_scr, *, n_valid, rows):
    c = DN_CHUNK
    step = pl.program_id(1)

    @pl.when(step == 0)
    def _():
        s_scr[...] = s0_ref[...]
        xp_scr[:, 0:SUBLANES, :] = cbuf_ref[...]

    @pl.when(step > 0)
    def _():
        xp_scr[:, 0:SUBLANES, :] = xp_scr[:, c:c + SUBLANES, :]

    xp_scr[:, SUBLANES:, :] = x_ref[...]

    def l2n(a):
        return a * lax.rsqrt(jnp.sum(a * a, axis=-1, keepdims=True) + NORM_EPS)

    def mm(a, b, dims=None):
        a, b = a.astype(BF16), b.astype(BF16)
        if dims is None:
            return jnp.dot(a, b, preferred_element_type=F32)
        return lax.dot_general(a, b, dims, preferred_element_type=F32)

    ri = lax.broadcasted_iota(jnp.int32, (c, c), 0)
    ci = lax.broadcasted_iota(jnp.int32, (c, c), 1)
    causal = ri >= ci
    strict = ri > ci
    eye = (ri == ci).astype(F32)
    tril_ones = causal.astype(F32)
    first = SUBLANES - (CONV_W - 1)

    for r in range(rows):
        def conv(col):
            lanes = slice(col, col + HEAD_DIM)
            y = xp_scr[r, first:first + c, lanes] * cw_ref[0:1, lanes]
            for j in range(1, CONV_W):
                y = y + xp_scr[r, first + j:first + j + c, lanes] * cw_ref[j:j + 1, lanes]
            return _silu(y)

        ba = ba_ref[r]
        beta = jax.nn.sigmoid(ba)
        xs = ba + dtb_ref[...]
        g = -jnp.exp(alog_ref[...]) * (jnp.maximum(xs, 0.0) + jnp.log1p(jnp.exp(-jnp.abs(xs))))
        if n_valid < c:
            live = lax.broadcasted_iota(jnp.int32, ba.shape, 0) < n_valid
            beta = jnp.where(live, beta, 0.0)
            g = jnp.where(live, g, 0.0)
        gc = jnp.dot(tril_ones, g, precision=HIGHEST, preferred_element_type=F32)
        gct = gc.T
        eg = jnp.exp(gc)
        g_last = gc[c - 1:c, :]
        kdf = jnp.exp(g_last - gc)
        g_tot = jnp.exp(g_last)

        hs = range(N_HEADS)
        col = lambda a, h: a[:, N_HEADS + h:N_HEADS + h + 1]
        qn = [l2n(conv(h * HEAD_DIM)) * (HEAD_DIM ** -0.5) for h in hs]
        kn = [l2n(conv(WIDTH + h * HEAD_DIM)) for h in hs]
        kb = [kn[h] * beta[:, h:h + 1] for h in hs]
        kk_qk = [mm(jnp.concatenate([kb[h], qn[h]], axis=0), kn[h], _NT) for h in hs]
        decay = [jnp.where(causal, jnp.exp(jnp.where(causal, col(gc, h) - gct[N_HEADS + h:N_HEADS + h + 1, :], 0.0)), 0.0)
                 for h in hs]
        a_qk = [jnp.where(causal, kk_qk[h][c:] * decay[h], 0.0) for h in hs]
        pw = [jnp.where(strict, -kk_qk[h][:c] * decay[h], 0.0) for h in hs]
        t_inv = [eye + pw[h] for h in hs]
        pw = [mm(pw[h], pw[h]) for h in hs]
        for _ in range(int(math.log2(c)) - 2):
            both = [mm(jnp.concatenate([t_inv[h], pw[h]], axis=0), pw[h]) for h in hs]
            t_inv = [t_inv[h] + both[h][:c] for h in hs]
            pw = [both[h][c:] for h in hs]
        t_inv = [t_inv[h] + mm(t_inv[h], pw[h]) for h in hs]
        vb_kbe = [jnp.concatenate([conv(2 * WIDTH + h * HEAD_DIM) * beta[:, h:h + 1], kb[h] * col(eg, h)], axis=1)
                  for h in hs]
        uw = [mm(t_inv[h], vb_kbe[h]) for h in hs]
        s_old = [s_scr[r, h] for h in hs]
        sq = [mm(jnp.concatenate([uw[h][:, HEAD_DIM:], qn[h] * col(eg, h)], axis=0), s_old[h]) for h in hs]
        v_new = [uw[h][:, :HEAD_DIM] - sq[h][:c] for h in hs]
        o = [sq[h][c:] + mm(a_qk[h], v_new[h]) for h in hs]
        kv = [mm(kn[h] * col(kdf, h), v_new[h], _TN) for h in hs]
        for h in hs:
            s_scr[r, h] = s_old[h] * col(g_tot, h) + kv[h]
            lanes = slice(h * HEAD_DIM, (h + 1) * HEAD_DIM)
            o_ref[r, :, lanes] = (_rms(o[h], nw_ref[...]) * _silu(z_ref[r, :, lanes])).astype(o_ref.dtype)

    @pl.when(step == pl.num_programs(1) - 1)
    def _():
        sout_ref[...] = s_scr[...]


def _deltanet(qkv, zba, conv_buf, state0, conv_w, a_log, dt_bias, norm_w, batch, seq, n_valid, rows):
    c = DN_CHUNK
    assert seq % c == 0 and (n_valid == c or seq == c) and batch % rows == 0
    n_chunks = seq // c
    cw3 = 3 * WIDTH
    cbuf = jnp.pad(conv_buf, ((0, 0), (SUBLANES - (CONV_W - 1), 0), (0, 0)))
    pad = (0, LANES - 2 * N_HEADS)
    alog = jnp.pad(jnp.concatenate([jnp.zeros_like(a_log), a_log]), pad).reshape(1, LANES)
    dtb = jnp.pad(jnp.concatenate([jnp.zeros_like(dt_bias), dt_bias]), pad).reshape(1, LANES)
    zba3 = zba.reshape(batch, seq, WIDTH + LANES)
    tok = lambda b, i: (b, i, 0)
    per_seq3 = lambda b, i: (b, 0, 0)
    per_seq4 = lambda b, i: (b, 0, 0, 0)
    fixed = lambda b, i: (0, 0)
    o, s_out = pl.pallas_call(
        functools.partial(_deltanet_body, n_valid=n_valid, rows=rows),
        grid=(batch // rows, n_chunks),
        in_specs=[
            pl.BlockSpec((rows, c, cw3), tok),
            pl.BlockSpec((rows, c, WIDTH), tok),
            pl.BlockSpec((rows, c, LANES), lambda b, i: (b, i, WIDTH // LANES)),
            pl.BlockSpec((rows, SUBLANES, cw3), per_seq3),
            pl.BlockSpec((CONV_W, cw3), fixed),
            pl.BlockSpec((1, LANES), fixed),
            pl.BlockSpec((1, LANES), fixed),
            pl.BlockSpec((1, HEAD_DIM), fixed),
            pl.BlockSpec((rows, N_HEADS, HEAD_DIM, HEAD_DIM), per_seq4),
        ],
        out_specs=[
            pl.BlockSpec((rows, c, WIDTH), tok),
            pl.BlockSpec((rows, N_HEADS, HEAD_DIM, HEAD_DIM), per_seq4),
        ],
        out_shape=[
            jax.ShapeDtypeStruct((batch, seq, WIDTH), BF16),
            jax.ShapeDtypeStruct((batch, N_HEADS, HEAD_DIM, HEAD_DIM), F32),
        ],
        scratch_shapes=[
            pltpu.VMEM((rows, c + SUBLANES, cw3), F32),
            pltpu.VMEM((rows, N_HEADS, HEAD_DIM, HEAD_DIM), F32),
        ],
        compiler_params=_params("arbitrary", "arbitrary"),
        name="deltanet",
    )(qkv.reshape(batch, seq, cw3), zba3, zba3, cbuf, conv_w, alog, dtb, norm_w.reshape(1, HEAD_DIM), state0)
    return o.reshape(batch * seq, WIDTH), s_out


def _out_proj_body(oa_ref, od_ref, x_ref, w_ref, wpost_ref, wpre_ref, x1_ref, h2_ref):
    mix = (jnp.dot(oa_ref[...], w_ref[0:WIDTH, :], preferred_element_type=F32)
           + jnp.dot(od_ref[...], w_ref[WIDTH:, :], preferred_element_type=F32))
    x1 = x_ref[...] + _rms(mix, wpost_ref[...])
    x1_ref[...] = x1
    h2_ref[...] = _rms(x1, wpre_ref[...]).astype(h2_ref.dtype)


def _out_proj(o_a, o_d, x, w_out, w_post, w_pre):
    t, d = x.shape
    tm = _row_tile(t, 512)
    row = lambda i: (i, 0)
    fixed = lambda i: (0, 0)
    return pl.pallas_call(
        _out_proj_body,
        grid=(t // tm,),
        in_specs=[
            pl.BlockSpec((tm, WIDTH), row),
            pl.BlockSpec((tm, WIDTH), row),
            pl.BlockSpec((tm, d), row),
            pl.BlockSpec((2 * WIDTH, d), fixed),
            pl.BlockSpec((1, d), fixed),
            pl.BlockSpec((1, d), fixed),
        ],
        out_specs=[pl.BlockSpec((tm, d), row), pl.BlockSpec((tm, d), row)],
        out_shape=[jax.ShapeDtypeStruct((t, d), F32), jax.ShapeDtypeStruct((t, d), BF16)],
        compiler_params=_params("arbitrary"),
        name="out_proj",
    )(o_a, o_d, x, w_out, w_post, w_pre)


def _mlp_body(pt_ref, h_ref, x1_ref, wu_ref, wd_ref, wpost_ref, *rest, n_pages, ppb):
    del pt_ref
    y_ref = rest[n_pages]
    f = pl.program_id(1)

    @pl.when(f == 0)
    def _():
        y_ref[...] = jnp.zeros_like(y_ref)

    a = jnp.dot(h_ref[...], wu_ref[...], preferred_element_type=F32)
    a = jnp.square(jnp.maximum(a, 0.0)).astype(BF16)
    d = y_ref.shape[1]
    nc = min(d, 2 * MXU_WIDTH)
    for c0 in range(0, d, nc):
        y_ref[:, c0:c0 + nc] += jnp.dot(a, wd_ref[:, c0:c0 + nc], preferred_element_type=F32)
    if n_pages:
        _sum_pages(rest[:n_pages], rest[n_pages + 1], ppb)

    @pl.when(f == pl.num_programs(1) - 1)
    def _():
        y_ref[...] = x1_ref[...] + _rms(y_ref[...], wpost_ref[...])


def _mlp_tiles(t, d_ff):
    tm, tf = _row_tile(t, 1024), min(512, d_ff)
    assert d_ff % tf == 0
    return tm, tf


def _mlp_pages_per_step(t, d_ff, total_pages, ppb):
    tm, tf = _mlp_tiles(t, d_ff)
    n_steps = (t // tm) * (d_ff // tf)
    per_step = total_pages // n_steps
    ok = per_step * n_steps == total_pages and per_step % ppb == 0 and 0 < per_step <= 2 * SUBLANES
    return per_step if ok else 0


def _mlp(h2, x1, w_up, w_down, w_post, page_job=None):
    t, d = x1.shape
    d_ff = w_up.shape[1]
    tm, tf = _mlp_tiles(t, d_ff)
    nf = d_ff // tf
    n_steps = (t // tm) * nf
    in_specs = [
        pl.BlockSpec((tm, d), lambda i, f, pt: (i, 0)),
        pl.BlockSpec((tm, d), lambda i, f, pt: (i, 0), pipeline_mode=pl.Buffered(1)),
        pl.BlockSpec((d, tf), lambda i, f, pt: (0, f)),
        pl.BlockSpec((tf, d), lambda i, f, pt: (f, 0)),
        pl.BlockSpec((1, d), lambda i, f, pt: (0, 0)),
    ]
    out_specs = [pl.BlockSpec((tm, d), lambda i, f, pt: (i, 0))]
    out_shape = [jax.ShapeDtypeStruct((t, d), F32)]
    args = [h2, x1, w_up, w_down, w_post]
    n_pages, ppb, pt = 0, 1, jnp.zeros((1,), jnp.int32)
    if page_job is not None:
        cache_k, pt, layer = page_job
        page = cache_k.shape[2]
        ppb = MOBA_BLOCK // page
        n_pages = _mlp_pages_per_step(t, d_ff, pt.shape[0], ppb)
        assert n_pages > 0
        in_specs += [pl.BlockSpec((None, None, page, N_HEADS, HEAD_DIM),
                                  lambda i, f, pt, r=r: (layer, pt[(i * nf + f) * n_pages + r], 0, 0, 0))
                     for r in range(n_pages)]
        args += [cache_k] * n_pages
        out_specs.append(pl.BlockSpec((None, n_pages // ppb, N_HEADS, HEAD_DIM), lambda i, f, pt: (i * nf + f, 0, 0, 0)))
        out_shape.append(jax.ShapeDtypeStruct((n_steps, n_pages // ppb, N_HEADS, HEAD_DIM), F32))
    res = pl.pallas_call(
        functools.partial(_mlp_body, n_pages=n_pages, ppb=ppb),
        grid_spec=pltpu.PrefetchScalarGridSpec(
            num_scalar_prefetch=1,
            grid=(t // tm, nf),
            in_specs=in_specs,
            out_specs=out_specs,
        ),
        out_shape=out_shape,
        compiler_params=_params("arbitrary", "arbitrary"),
        name="mlp",
    )(pt, *args)
    return res if page_job is not None else res[0]


def _rope_tables(pos, rows, scale):
    half = HEAD_DIM // 2
    inv_freq = 1.0 / (ROPE_THETA ** (jnp.arange(half, dtype=F32) / half))
    ang = pos.astype(F32)[:, None] * inv_freq[None, :]
    cos = jnp.concatenate([jnp.cos(ang), jnp.cos(ang)], axis=1) * scale
    sin = jnp.concatenate([-jnp.sin(ang), jnp.sin(ang)], axis=1) * scale
    reps = max(1, rows // pos.shape[0])
    return jnp.tile(cos, (reps, 1)), jnp.tile(sin, (reps, 1))


def _layer(x, pos, conv_buf, dn_state, moba_fn, lw, *, q_dtype, page_job=None):
    (w_pre_mix, w_q, w_k, w_v, w_d, w_zba, conv_w, a_log, dt_bias, dn_norm_w, w_out, w_post_mix, w_pre_mlp,
     w_up, w_down, w_post_mlp) = lw
    batch, seq, d = x.shape
    t = batch * seq
    x2 = x.reshape(t, d)
    tm = _row_tile(t, 1024)
    assert tm % seq == 0 or seq % tm == 0
    q, h = _proj(x2, w_q, out_dtype=q_dtype, rope=_rope_tables(pos, tm, ATTN_SCALE), norm_w=w_pre_mix.reshape(1, d))
    k = _proj(h, w_k, out_dtype=F32, rope=_rope_tables(pos, tm, 1.0))
    v = _proj(h, w_v, out_dtype=F32)
    qkv_d = _proj(h, w_d, out_dtype=F32)
    zba = _proj(h, w_zba, out_dtype=F32)
    o_a = moba_fn(q, k, v)

    c = DN_CHUNK
    seq_pad = -(-seq // c) * c
    if seq_pad == seq:
        dn_in, n_valid = (qkv_d, zba), c
    else:
        assert seq < c
        padrows = lambda a: jnp.pad(a.reshape(batch, seq, -1), ((0, 0), (0, c - seq), (0, 0))).reshape(batch * c, -1)
        dn_in, n_valid = (padrows(qkv_d), padrows(zba)), seq
    o_d, new_state = _deltanet(*dn_in, conv_buf, dn_state, conv_w, a_log, dt_bias, dn_norm_w, batch, seq_pad, n_valid,
                               rows=DN_ROWS)
    if seq_pad != seq:
        o_d = o_d.reshape(batch, c, WIDTH)[:, :seq].reshape(t, WIDTH)
    keep = CONV_W - 1
    tail = qkv_d.reshape(batch, seq, -1)[:, max(0, seq - keep):]
    new_buf = tail if seq >= keep else jnp.concatenate([conv_buf, tail], axis=1)[:, -keep:]

    x1, h2 = _out_proj(o_a, o_d, x2, w_out, w_post_mix.reshape(1, d), w_pre_mlp.reshape(1, d))
    kmean = None
    if page_job is not None and _mlp_pages_per_step(t, w_up.shape[1], page_job[1].shape[0],
                                                    MOBA_BLOCK // page_job[0].shape[2]):
        y, kmean = _mlp(h2, x1, w_up, w_down, w_post_mlp.reshape(1, d), page_job)
    else:
        y = _mlp(h2, x1, w_up, w_down, w_post_mlp.reshape(1, d))
    heads = (batch, seq, N_HEADS, HEAD_DIM)
    return y.reshape(batch, seq, d), k.reshape(heads), v.reshape(heads), new_buf, new_state, kmean


def kernel(x_prompt, x_sample, cache_k, cache_v, state_dn, state_conv, page_table, w_pre_mix, w_in, conv_w,
           a_log, dt_bias, dn_norm_w, w_out, w_post_mix, w_pre_mlp, w_up, w_down, w_post_mlp):
    depth = w_in.shape[0]
    batch, seq, _ = x_prompt.shape
    dec_batch, dec_seq, _ = x_sample.shape
    past = page_table.shape[1] * cache_k.shape[2]
    pos_p = jnp.arange(seq, dtype=jnp.int32)
    pos_s = past + jnp.arange(dec_seq, dtype=jnp.int32)
    cuts = [0, WIDTH, 2 * WIDTH, 3 * WIDTH, 6 * WIDTH, 7 * WIDTH]
    hp, hs = x_prompt, x_sample
    outs = [[] for _ in range(8)]
    for l in range(depth):
        w_slices = [w_in[l][:, a:b].astype(BF16) for a, b in zip(cuts[:-1], cuts[1:])]
        w_ba = jnp.pad(w_in[l][:, cuts[-1]:], ((0, 0), (0, LANES - 2 * N_HEADS))).astype(BF16)
        w_slices[-1] = jnp.concatenate([w_slices[-1], w_ba], axis=1)
        lw = (w_pre_mix[l], *w_slices, conv_w[l], a_log[l], dt_bias[l], dn_norm_w[l],
              w_out[l].astype(BF16), w_post_mix[l], w_pre_mlp[l], w_up[l].astype(BF16), w_down[l].astype(BF16),
              w_post_mlp[l])
        zero_buf = jnp.zeros((batch, CONV_W - 1, 3 * WIDTH), x_prompt.dtype)
        zero_state = jnp.zeros((batch, N_HEADS, HEAD_DIM, HEAD_DIM), state_dn.dtype)
        prompt_moba = functools.partial(_moba_prompt, batch=batch, seq=seq)
        hp, kp, vp, cvp, dnp, kmean = _layer(hp, pos_p, zero_buf, zero_state, prompt_moba, lw, q_dtype=BF16,
                                             page_job=(cache_k, page_table.reshape(-1), l))
        if kmean is not None:
            kmean = kmean.reshape(dec_batch, -1, N_HEADS, HEAD_DIM)
        sample_moba = functools.partial(_moba_sample, cache_k=cache_k, cache_v=cache_v, page_table=page_table,
                                        layer=l, dec_batch=dec_batch, dec_seq=dec_seq, kmean=kmean)
        hs, kss, vss, cvs, dns, _ = _layer(hs, pos_s, state_conv[l], state_dn[l], sample_moba, lw, q_dtype=F32)
        for lst, val in zip(outs, (kp, vp, kss, vss, dnp, dns, cvp, cvs)):
            lst.append(val)
    return (hp, hs) + tuple(jnp.stack(o) for o in outs)
```

```python
import functools
import math

import jax
import jax.numpy as jnp
from jax import lax
from jax.experimental import pallas as pl
from jax.experimental.pallas import tpu as pltpu

F32 = jnp.float32
BF16 = jnp.bfloat16
HIGHEST = lax.Precision.HIGHEST

HEAD_DIM = 128
N_HEADS = 8
WIDTH = N_HEADS * HEAD_DIM
MOBA_BLOCK = 256
MOBA_TOPK = 3
DN_CHUNK = 64
DN_ROWS = 4
CONV_W = 4
ROPE_THETA = 10000.0
NORM_EPS = 1e-6
ATTN_SCALE = HEAD_DIM ** -0.5
LANES = 128
SUBLANES = 8
MXU_WIDTH = 256
VMEM_LIMIT = 56 * 1024 * 1024

_NT = (((1,), (1,)), ((), ()))
_TN = (((0,), (0,)), ((), ()))


def _params(*sem):
    return pltpu.CompilerParams(dimension_semantics=sem, vmem_limit_bytes=VMEM_LIMIT)


def _rms(x, w):
    return x * lax.rsqrt(jnp.mean(x * x, axis=-1, keepdims=True) + NORM_EPS) * w


def _silu(x):
    return x * jax.nn.sigmoid(x)


def _row_tile(t, cap):
    tm = min(t, cap)
    assert t % tm == 0
    return tm


def _proj_body(h_ref, w_ref, *rest, rope, tn, fused_norm):
    if fused_norm:
        wn_ref, rest, hn_ref = rest[0], rest[1:-1], rest[-1]
        h = _rms(h_ref[...], wn_ref[...]).astype(BF16)
        hn_ref[...] = h
    else:
        h = h_ref[...]
    o_ref = rest[-1]
    for c0 in range(0, tn, MXU_WIDTH):
        chunk = min(MXU_WIDTH, tn - c0)
        acc = jnp.dot(h, w_ref[:, c0:c0 + chunk], preferred_element_type=F32)
        if rope:
            cos, sin = rest[0][...], rest[1][...]
            acc = jnp.concatenate(
                [acc[:, c:c + HEAD_DIM] * cos + pltpu.roll(acc[:, c:c + HEAD_DIM], HEAD_DIM // 2, 1) * sin
                 for c in range(0, chunk, HEAD_DIM)], axis=1)
        o_ref[:, c0:c0 + chunk] = acc.astype(o_ref.dtype)


def _proj(h, w, *, out_dtype, cols=None, rope=None, norm_w=None):
    t, d = h.shape
    col0, col1 = (0, w.shape[1]) if cols is None else cols
    n = col1 - col0
    tm = _row_tile(t, 1024)
    tn = n if n <= 1536 else 1024
    assert n % tn == 0 and tn % LANES == 0 and col0 % tn == 0
    first_col_block = col0 // tn
    fused_norm = norm_w is not None
    assert not fused_norm or n == tn
    in_specs = [pl.BlockSpec((tm, d), lambda i, j: (i, 0)),
                pl.BlockSpec((d, tn), lambda i, j: (0, first_col_block + j))]
    args = [h, w]
    if fused_norm:
        in_specs.append(pl.BlockSpec((1, d), lambda i, j: (0, 0)))
        args.append(norm_w)
    if rope is not None:
        n_rope = rope[0].shape[0] // tm
        in_specs += [pl.BlockSpec((tm, HEAD_DIM), lambda i, j: (i % n_rope, 0))] * 2
        args += list(rope)
    out_specs = [pl.BlockSpec((tm, tn), lambda i, j: (i, j))]
    out_shape = [jax.ShapeDtypeStruct((t, n), out_dtype)]
    if fused_norm:
        out_specs.append(pl.BlockSpec((tm, d), lambda i, j: (i, 0)))
        out_shape.append(jax.ShapeDtypeStruct((t, d), BF16))
    res = pl.pallas_call(
        functools.partial(_proj_body, rope=rope is not None, tn=tn, fused_norm=fused_norm),
        grid=(t // tm, n // tn),
        in_specs=in_specs,
        out_specs=out_specs,
        out_shape=out_shape,
        compiler_params=_params("arbitrary", "arbitrary"),
        name=("norm_" if fused_norm else "") + ("proj_rope" if rope is not None else "proj"),
    )(*args)
    return res if fused_norm else res[0]


def _topk_mask(gate, valid, axis):
    n = gate.shape[axis]
    idx = lax.broadcasted_iota(jnp.int32, gate.shape, axis)
    g = jnp.where(valid, gate, -jnp.inf)
    sel = jnp.zeros(gate.shape, jnp.bool_)
    for _ in range(min(MOBA_TOPK, n)):
        m = jnp.max(g, axis=axis, keepdims=True)
        first = jnp.min(jnp.where(g == m, idx, n), axis=axis, keepdims=True)
        hit = idx == first
        sel = sel | hit
        g = jnp.where(hit, -jnp.inf, g)
    return sel & valid


def _moba_prompt_body(q_ref, k_ref, v_ref, o_ref, kb_scr, vt_scr, km_scr, sel_scr, m_scr, l_scr, acc_scr, *, nb):
    blk = MOBA_BLOCK
    qi = pl.program_id(1)
    heads = [slice(h * HEAD_DIM, (h + 1) * HEAD_DIM) for h in range(N_HEADS)]

    @pl.when(qi == 0)
    def _():
        for h in range(N_HEADS):
            for j in range(nb):
                rows = slice(j * blk, (j + 1) * blk)
                k = k_ref[rows, heads[h]]
                kb_scr[h, rows, :] = k.astype(BF16)
                vt_scr[h, :, rows] = v_ref[rows, heads[h]].T.astype(BF16)
                km_scr[h, j:j + 1, :] = jnp.sum(k, axis=0, keepdims=True) / blk

    own = pl.multiple_of(qi * blk, blk)
    kpos = lax.broadcasted_iota(jnp.int32, (blk, blk), 0)
    qpos = lax.broadcasted_iota(jnp.int32, (blk, blk), 1)
    block_id = lax.broadcasted_iota(jnp.int32, (nb, blk), 0)
    hs = range(N_HEADS)
    gate = [lax.dot_general(km_scr[h], q_ref[:, heads[h]].astype(F32), _NT, precision=HIGHEST,
                            preferred_element_type=F32) for h in hs]
    s = [lax.dot_general(kb_scr[h, pl.ds(own, blk), :], q_ref[:, heads[h]], _NT, preferred_element_type=F32)
         for h in hs]
    p = []
    for h in hs:
        sel_scr[h] = _topk_mask(gate[h], block_id < qi, 0).astype(F32)
        s_h = jnp.where(kpos <= qpos, s[h], -jnp.inf)
        m = jnp.max(s_h, axis=0, keepdims=True)
        p_h = jnp.exp(s_h - m)
        m_scr[h] = m
        l_scr[h] = jnp.sum(p_h, axis=0, keepdims=True)
        p.append(p_h.astype(BF16))
    for h in hs:
        acc_scr[h] = jnp.dot(vt_scr[h, :, pl.ds(own, blk)], p[h], preferred_element_type=F32)

    def past_block(j, carry):
        off = pl.multiple_of(j * blk, blk)

        s = [lax.dot_general(kb_scr[h, pl.ds(off, blk), :], q_ref[:, heads[h]], _NT, preferred_element_type=F32)
             for h in hs]
        p, alpha = [], []
        for h in hs:
            s_h = jnp.where(sel_scr[h, pl.ds(j, 1), :] > 0, s[h], -jnp.inf)
            m_old = m_scr[h]
            m_new = jnp.maximum(m_old, jnp.max(s_h, axis=0, keepdims=True))
            a_h = jnp.exp(m_old - m_new)
            p_h = jnp.exp(s_h - m_new)
            m_scr[h] = m_new
            l_scr[h] = a_h * l_scr[h] + jnp.sum(p_h, axis=0, keepdims=True)
            p.append(p_h.astype(BF16))
            alpha.append(a_h)
        pv = [jnp.dot(vt_scr[h, :, pl.ds(off, blk)], p[h], preferred_element_type=F32) for h in hs]
        for h in hs:
            acc_scr[h] = alpha[h] * acc_scr[h] + pv[h]
        return carry

    lax.fori_loop(0, qi, past_block, 0)
    for h in range(N_HEADS):
        o_ref[:, heads[h]] = (acc_scr[h] / l_scr[h]).T.astype(o_ref.dtype)


def _moba_prompt(q, k, v, batch, seq):
    assert seq % MOBA_BLOCK == 0
    nb = seq // MOBA_BLOCK
    t = batch * seq
    return pl.pallas_call(
        functools.partial(_moba_prompt_body, nb=nb),
        grid=(batch, nb),
        in_specs=[
            pl.BlockSpec((MOBA_BLOCK, WIDTH), lambda b, i: (b * nb + i, 0)),
            pl.BlockSpec((seq, WIDTH), lambda b, i: (b, 0)),
            pl.BlockSpec((seq, WIDTH), lambda b, i: (b, 0)),
        ],
        out_specs=pl.BlockSpec((MOBA_BLOCK, WIDTH), lambda b, i: (b * nb + i, 0)),
        out_shape=jax.ShapeDtypeStruct((t, WIDTH), BF16),
        scratch_shapes=[
            pltpu.VMEM((N_HEADS, seq, HEAD_DIM), BF16),
            pltpu.VMEM((N_HEADS, HEAD_DIM, seq), BF16),
            pltpu.VMEM((N_HEADS, nb, HEAD_DIM), F32),
            pltpu.VMEM((N_HEADS, nb, MOBA_BLOCK), F32),
            pltpu.VMEM((N_HEADS, 1, MOBA_BLOCK), F32),
            pltpu.VMEM((N_HEADS, 1, MOBA_BLOCK), F32),
            pltpu.VMEM((N_HEADS, HEAD_DIM, MOBA_BLOCK), F32),
        ],
        compiler_params=_params("arbitrary", "arbitrary"),
        name="moba_prompt",
    )(q, k, v)


def _sum_pages(page_refs, out_ref, ppb):
    for i in range(len(page_refs) // ppb):
        tot = jnp.sum(page_refs[i * ppb][...], axis=0)
        for r in range(1, ppb):
            tot = tot + jnp.sum(page_refs[i * ppb + r][...], axis=0)
        out_ref[i] = tot / MOBA_BLOCK


def _page_sum_body(pt_ref, *refs, n_in, ppb):
    del pt_ref
    _sum_pages(refs[:n_in], refs[n_in], ppb)


def _block_means(cache_k, page_table_flat, layer, dec_batch, n_pages):
    page = cache_k.shape[2]
    ppb = MOBA_BLOCK // page
    assert n_pages % ppb == 0
    per_step = ppb * SUBLANES
    while n_pages % per_step:
        per_step //= 2
    assert per_step % ppb == 0
    steps = n_pages // per_step
    nb = n_pages // ppb

    def spec(r):
        return pl.BlockSpec((None, None, page, N_HEADS, HEAD_DIM),
                            lambda b, p, pt, r=r: (layer, pt[b * n_pages + p * per_step + r], 0, 0, 0))

    return pl.pallas_call(
        functools.partial(_page_sum_body, n_in=per_step, ppb=ppb),
        grid_spec=pltpu.PrefetchScalarGridSpec(
            num_scalar_prefetch=1,
            grid=(dec_batch, steps),
            in_specs=[spec(r) for r in range(per_step)],
            out_specs=pl.BlockSpec((None, per_step // ppb, N_HEADS, HEAD_DIM), lambda b, p, pt: (b, p, 0, 0)),
        ),
        out_shape=jax.ShapeDtypeStruct((dec_batch, nb, N_HEADS, HEAD_DIM), F32),
        compiler_params=_params("arbitrary", "arbitrary"),
        name="moba_block_means",
    )(page_table_flat, *([cache_k] * per_step))


def _sample_select_body(q_ref, km_ref, idx_ref, *, dec_seq):
    for h in range(N_HEADS):
        q = q_ref[:, h * HEAD_DIM:(h + 1) * HEAD_DIM]
        gate = lax.dot_general(q, km_ref[:, h, :], _NT, precision=HIGHEST, preferred_element_type=F32)
        nb = gate.shape[1]
        idx = lax.broadcasted_iota(jnp.int32, gate.shape, 1)
        lane = lax.broadcasted_iota(jnp.int32, (dec_seq, LANES), 1)
        out = jnp.zeros((dec_seq, LANES), jnp.int32)
        g = gate
        for r in range(MOBA_TOPK):
            m = jnp.max(g, axis=1, keepdims=True)
            first = jnp.min(jnp.where(g == m, idx, nb), axis=1, keepdims=True)
            out = jnp.where(lane == r, first, out)
            g = jnp.where(idx == first, -jnp.inf, g)
        idx_ref[h] = out


def _sample_select(q, kmean, dec_batch, dec_seq):
    nb = kmean.shape[1]
    return pl.pallas_call(
        functools.partial(_sample_select_body, dec_seq=dec_seq),
        grid=(dec_batch,),
        in_specs=[
            pl.BlockSpec((dec_seq, WIDTH), lambda b: (b, 0)),
            pl.BlockSpec((None, nb, N_HEADS, HEAD_DIM), lambda b: (b, 0, 0, 0)),
        ],
        out_specs=pl.BlockSpec((None, N_HEADS, dec_seq, LANES), lambda b: (b, 0, 0, 0)),
        out_shape=jax.ShapeDtypeStruct((dec_batch, N_HEADS, dec_seq, LANES), jnp.int32),
        compiler_params=_params("arbitrary"),
        name="moba_sample_select",
    )(q, kmean)


def _sample_attn_body(idx_ref, pt_ref, q_ref, kn_ref, vn_ref, ck_hbm, cv_hbm, o_ref, kbuf, vbuf, sem,
                      *, layer, dec_seq, ppb, page, n_pages):
    n_sel = MOBA_TOPK * ppb
    step = pl.program_id(0) * N_HEADS + pl.program_id(1)
    n_steps = pl.num_programs(0) * N_HEADS
    slot = step % 2

    def slab_copies(at_step, at_slot, qi, s):
        head = at_step % N_HEADS
        block = idx_ref[(at_step * dec_seq + qi) * MOBA_TOPK + s // ppb]
        phys = pt_ref[(at_step // N_HEADS) * n_pages + block * ppb + s % ppb]
        dst = pl.ds(s * page, page)
        return (pltpu.make_async_copy(ck_hbm.at[layer, phys, :, head, :], kbuf.at[at_slot, qi, dst, :], sem.at[0, at_slot]),
                pltpu.make_async_copy(cv_hbm.at[layer, phys, :, head, :], vbuf.at[at_slot, qi, dst, :], sem.at[1, at_slot]))

    def for_all_slabs(at_step, at_slot, action):
        for qi in range(dec_seq):
            for s in range(n_sel):
                for c in slab_copies(at_step, at_slot, qi, s):
                    action(c)

    @pl.when(step == 0)
    def _():
        for_all_slabs(step, slot, lambda c: c.start())

    @pl.when(step + 1 < n_steps)
    def _():
        for_all_slabs(step + 1, 1 - slot, lambda c: c.start())

    for_all_slabs(step, slot, lambda c: c.wait())

    qb = q_ref[...].astype(BF16)
    vn = vn_ref[...]
    s_own = lax.dot_general(kn_ref[...].astype(BF16), qb, _NT, preferred_element_type=F32)
    kpos = lax.broadcasted_iota(jnp.int32, s_own.shape, 0)
    qpos = lax.broadcasted_iota(jnp.int32, s_own.shape, 1)
    s_own = jnp.where(kpos <= qpos, s_own, -jnp.inf)
    rows = []
    for qi in range(dec_seq):
        s_all = lax.dot_general(kbuf[slot, qi].astype(BF16), qb, _NT, preferred_element_type=F32)
        s_sel = s_all[:, qi:qi + 1]
        s_new = s_own[:, qi:qi + 1]
        m = jnp.maximum(jnp.max(s_sel, axis=0, keepdims=True), jnp.max(s_new, axis=0, keepdims=True))
        p_sel = jnp.exp(s_sel - m)
        p_new = jnp.exp(s_new - m)
        l = jnp.sum(p_sel, axis=0, keepdims=True) + jnp.sum(p_new, axis=0, keepdims=True)
        pv = jnp.sum(p_sel * vbuf[slot, qi], axis=0, keepdims=True) + jnp.sum(p_new * vn, axis=0, keepdims=True)
        rows.append(pv / l)
    o_ref[...] = jnp.concatenate(rows, axis=0).astype(o_ref.dtype)


def _sample_attn(idx_flat, pt_flat, q, k_new, v_new, cache_k, cache_v, layer, dec_batch, dec_seq):
    page = cache_k.shape[2]
    ppb = MOBA_BLOCK // page
    n_keys = MOBA_TOPK * MOBA_BLOCK
    row = lambda b, h, idx, pt: (b, h)
    return pl.pallas_call(
        functools.partial(_sample_attn_body, layer=layer, dec_seq=dec_seq, ppb=ppb, page=page,
                          n_pages=pt_flat.shape[0] // dec_batch),
        grid_spec=pltpu.PrefetchScalarGridSpec(
            num_scalar_prefetch=2,
            grid=(dec_batch, N_HEADS),
            in_specs=[
                pl.BlockSpec((dec_seq, HEAD_DIM), row),
                pl.BlockSpec((dec_seq, HEAD_DIM), row),
                pl.BlockSpec((dec_seq, HEAD_DIM), row),
                pl.BlockSpec(memory_space=pl.ANY),
                pl.BlockSpec(memory_space=pl.ANY),
            ],
            out_specs=pl.BlockSpec((dec_seq, HEAD_DIM), row),
            scratch_shapes=[
                pltpu.VMEM((2, dec_seq, n_keys, HEAD_DIM), F32),
                pltpu.VMEM((2, dec_seq, n_keys, HEAD_DIM), F32),
                pltpu.SemaphoreType.DMA((2, 2)),
            ],
        ),
        out_shape=jax.ShapeDtypeStruct((dec_batch * dec_seq, WIDTH), F32),
        compiler_params=_params("arbitrary", "arbitrary"),
        name="moba_sample_attn",
    )(idx_flat, pt_flat, q, k_new, v_new, cache_k, cache_v)


def _moba_sample(q, k_new, v_new, cache_k, cache_v, page_table, layer, dec_batch, dec_seq, kmean=None):
    n_pages = page_table.shape[1]
    page = cache_k.shape[2]
    ppb = MOBA_BLOCK // page
    past = n_pages * page
    assert past % MOBA_BLOCK == 0 and dec_seq <= MOBA_BLOCK and past // MOBA_BLOCK >= MOBA_TOPK
    assert dec_seq % SUBLANES == 0
    pt_flat = page_table.reshape(-1)
    if kmean is None:
        kmean = _block_means(cache_k, pt_flat, layer, dec_batch, n_pages)
    idx = _sample_select(q, kmean, dec_batch, dec_seq)[..., :MOBA_TOPK]
    o = _sample_attn(idx.reshape(-1), pt_flat, q, k_new, v_new, cache_k, cache_v, layer, dec_batch, dec_seq)
    return o.astype(BF16)


def _deltanet_body(x_ref, z_ref, ba_ref, cbuf_ref, cw_ref, alog_ref, dtb_ref, nw_ref, s0_ref,
                   o_ref, sout_ref, xp_scr, s_scr, *, n_valid, rows):
    c = DN_CHUNK
    step = pl.program_id(1)

    @pl.when(step == 0)
    def _():
        s_scr[...] = s0_ref[...]
        xp_scr[:, 0:SUBLANES, :] = cbuf_ref[...]

    @pl.when(step > 0)
    def _():
        xp_scr[:, 0:SUBLANES, :] = xp_scr[:, c:c + SUBLANES, :]

    xp_scr[:, SUBLANES:, :] = x_ref[...]

    def l2n(a):
        return a * lax.rsqrt(jnp.sum(a * a, axis=-1, keepdims=True) + NORM_EPS)

    def mm(a, b, dims=None):
        a, b = a.astype(BF16), b.astype(BF16)
        if dims is None:
            return jnp.dot(a, b, preferred_element_type=F32)
        return lax.dot_general(a, b, dims, preferred_element_type=F32)

    ri = lax.broadcasted_iota(jnp.int32, (c, c), 0)
    ci = lax.broadcasted_iota(jnp.int32, (c, c), 1)
    causal = ri >= ci
    strict = ri > ci
    eye = (ri == ci).astype(F32)
    tril_ones = causal.astype(F32)
    first = SUBLANES - (CONV_W - 1)

    for r in range(rows):
        def conv(col):
            lanes = slice(col, col + HEAD_DIM)
            y = xp_scr[r, first:first + c, lanes] * cw_ref[0:1, lanes]
            for j in range(1, CONV_W):
                y = y + xp_scr[r, first + j:first + j + c, lanes] * cw_ref[j:j + 1, lanes]
            return _silu(y)

        ba = ba_ref[r]
        beta = jax.nn.sigmoid(ba)
        xs = ba + dtb_ref[...]
        g = -jnp.exp(alog_ref[...]) * (jnp.maximum(xs, 0.0) + jnp.log1p(jnp.exp(-jnp.abs(xs))))
        if n_valid < c:
            live = lax.broadcasted_iota(jnp.int32, ba.shape, 0) < n_valid
            beta = jnp.where(live, beta, 0.0)
            g = jnp.where(live, g, 0.0)
        gc = jnp.dot(tril_ones, g, precision=HIGHEST, preferred_element_type=F32)
        gct = gc.T
        eg = jnp.exp(gc)
        g_last = gc[c - 1:c, :]
        kdf = jnp.exp(g_last - gc)
        g_tot = jnp.exp(g_last)

        hs = range(N_HEADS)
        col = lambda a, h: a[:, N_HEADS + h:N_HEADS + h + 1]
        qn = [l2n(conv(h * HEAD_DIM)) * (HEAD_DIM ** -0.5) for h in hs]
        kn = [l2n(conv(WIDTH + h * HEAD_DIM)) for h in hs]
        kb = [kn[h] * beta[:, h:h + 1] for h in hs]
        kk_qk = [mm(jnp.concatenate([kb[h], qn[h]], axis=0), kn[h], _NT) for h in hs]
        decay = [jnp.where(causal, jnp.exp(jnp.where(causal, col(gc, h) - gct[N_HEADS + h:N_HEADS + h + 1, :], 0.0)), 0.0)
                 for h in hs]
        a_qk = [jnp.where(causal, kk_qk[h][c:] * decay[h], 0.0) for h in hs]
        pw = [jnp.where(strict, -kk_qk[h][:c] * decay[h], 0.0) for h in hs]
        t_inv = [eye + pw[h] for h in hs]
        pw = [mm(pw[h], pw[h]) for h in hs]
        for _ in range(int(math.log2(c)) - 2):
            both = [mm(jnp.concatenate([t_inv[h], pw[h]], axis=0), pw[h]) for h in hs]
            t_inv = [t_inv[h] + both[h][:c] for h in hs]
            pw = [both[h][c:] for h in hs]
        t_inv = [t_inv[h] + mm(t_inv[h], pw[h]) for h in hs]
        vb_kbe = [jnp.concatenate([conv(2 * WIDTH + h * HEAD_DIM) * beta[:, h:h + 1], kb[h] * col(eg, h)], axis=1)
                  for h in hs]
        uw = [mm(t_inv[h], vb_kbe[h]) for h in hs]
        s_old = [s_scr[r, h] for h in hs]
        sq = [mm(jnp.concatenate([uw[h][:, HEAD_DIM:], qn[h] * col(eg, h)], axis=0), s_old[h]) for h in hs]
        v_new = [uw[h][:, :HEAD_DIM] - sq[h][:c] for h in hs]
        o = [sq[h][c:] + mm(a_qk[h], v_new[h]) for h in hs]
        kv = [mm(kn[h] * col(kdf, h), v_new[h], _TN) for h in hs]
        for h in hs:
            s_scr[r, h] = s_old[h] * col(g_tot, h) + kv[h]
            lanes = slice(h * HEAD_DIM, (h + 1) * HEAD_DIM)
            o_ref[r, :, lanes] = (_rms(o[h], nw_ref[...]) * _silu(z_ref[r, :, lanes])).astype(o_ref.dtype)

    @pl.when(step == pl.num_programs(1) - 1)
    def _():
        sout_ref[...] = s_scr[...]


def _deltanet(qkv, zba, conv_buf, state0, conv_w, a_log, dt_bias, norm_w, batch, seq, n_valid, rows):
    c = DN_CHUNK
    assert seq % c == 0 and (n_valid == c or seq == c) and batch % rows == 0
    n_chunks = seq // c
    cw3 = 3 * WIDTH
    cbuf = jnp.pad(conv_buf, ((0, 0), (SUBLANES - (CONV_W - 1), 0), (0, 0)))
    pad = (0, LANES - 2 * N_HEADS)
    alog = jnp.pad(jnp.concatenate([jnp.zeros_like(a_log), a_log]), pad).reshape(1, LANES)
    dtb = jnp.pad(jnp.concatenate([jnp.zeros_like(dt_bias), dt_bias]), pad).reshape(1, LANES)
    zba3 = zba.reshape(batch, seq, WIDTH + LANES)
    tok = lambda b, i: (b, i, 0)
    per_seq3 = lambda b, i: (b, 0, 0)
    per_seq4 = lambda b, i: (b, 0, 0, 0)
    fixed = lambda b, i: (0, 0)
    o, s_out = pl.pallas_call(
        functools.partial(_deltanet_body, n_valid=n_valid, rows=rows),
        grid=(batch // rows, n_chunks),
        in_specs=[
            pl.BlockSpec((rows, c, cw3), tok),
            pl.BlockSpec((rows, c, WIDTH), tok),
            pl.BlockSpec((rows, c, LANES), lambda b, i: (b, i, WIDTH // LANES)),
            pl.BlockSpec((rows, SUBLANES, cw3), per_seq3),
            pl.BlockSpec((CONV_W, cw3), fixed),
            pl.BlockSpec((1, LANES), fixed),
            pl.BlockSpec((1, LANES), fixed),
            pl.BlockSpec((1, HEAD_DIM), fixed),
            pl.BlockSpec((rows, N_HEADS, HEAD_DIM, HEAD_DIM), per_seq4),
        ],
        out_specs=[
            pl.BlockSpec((rows, c, WIDTH), tok),
            pl.BlockSpec((rows, N_HEADS, HEAD_DIM, HEAD_DIM), per_seq4),
        ],
        out_shape=[
            jax.ShapeDtypeStruct((batch, seq, WIDTH), BF16),
            jax.ShapeDtypeStruct((batch, N_HEADS, HEAD_DIM, HEAD_DIM), F32),
        ],
        scratch_shapes=[
            pltpu.VMEM((rows, c + SUBLANES, cw3), F32),
            pltpu.VMEM((rows, N_HEADS, HEAD_DIM, HEAD_DIM), F32),
        ],
        compiler_params=_params("arbitrary", "arbitrary"),
        name="deltanet",
    )(qkv.reshape(batch, seq, cw3), zba3, zba3, cbuf, conv_w, alog, dtb, norm_w.reshape(1, HEAD_DIM), state0)
    return o.reshape(batch * seq, WIDTH), s_out


def _out_proj_body(oa_ref, od_ref, x_ref, w_ref, wpost_ref, wpre_ref, x1_ref, h2_ref):
    tm = x_ref.shape[0]
    sub = min(tm, MXU_WIDTH)
    for r0 in range(0, tm, sub):
        rows = slice(r0, r0 + sub)
        mix = (jnp.dot(oa_ref[rows, :], w_ref[0:WIDTH, :], preferred_element_type=F32)
               + jnp.dot(od_ref[rows, :], w_ref[WIDTH:, :], preferred_element_type=F32))
        x1 = x_ref[rows, :] + _rms(mix, wpost_ref[...])
        x1_ref[rows, :] = x1
        h2_ref[rows, :] = _rms(x1, wpre_ref[...]).astype(h2_ref.dtype)


def _out_proj(o_a, o_d, x, w_out, w_post, w_pre):
    t, d = x.shape
    tm = _row_tile(t, 512)
    row = lambda i: (i, 0)
    fixed = lambda i: (0, 0)
    return pl.pallas_call(
        _out_proj_body,
        grid=(t // tm,),
        in_specs=[
            pl.BlockSpec((tm, WIDTH), row),
            pl.BlockSpec((tm, WIDTH), row),
            pl.BlockSpec((tm, d), row),
            pl.BlockSpec((2 * WIDTH, d), fixed),
            pl.BlockSpec((1, d), fixed),
            pl.BlockSpec((1, d), fixed),
        ],
        out_specs=[pl.BlockSpec((tm, d), row), pl.BlockSpec((tm, d), row)],
        out_shape=[jax.ShapeDtypeStruct((t, d), F32), jax.ShapeDtypeStruct((t, d), BF16)],
        compiler_params=_params("arbitrary"),
        name="out_proj",
    )(o_a, o_d, x, w_out, w_post, w_pre)


def _mlp_body(pt_ref, h_ref, x1_ref, wu_ref, wd_ref, wpost_ref, *rest, n_pages, ppb):
    del pt_ref
    y_ref = rest[n_pages]
    f = pl.program_id(1)

    @pl.when(f == 0)
    def _():
        y_ref[...] = jnp.zeros_like(y_ref)

    a = jnp.dot(h_ref[...], wu_ref[...], preferred_element_type=F32)
    a = jnp.square(jnp.maximum(a, 0.0)).astype(BF16)
    y_ref[...] += jnp.dot(a, wd_ref[...], preferred_element_type=F32)
    if n_pages:
        _sum_pages(rest[:n_pages], rest[n_pages + 1], ppb)

    @pl.when(f == pl.num_programs(1) - 1)
    def _():
        y_ref[...] = x1_ref[...] + _rms(y_ref[...], wpost_ref[...])


def _mlp_tiles(t, d_ff):
    tm, tf = _row_tile(t, 512), min(1024, d_ff)
    assert d_ff % tf == 0
    return tm, tf


def _mlp_pages_per_step(t, d_ff, total_pages, ppb):
    tm, tf = _mlp_tiles(t, d_ff)
    n_steps = (t // tm) * (d_ff // tf)
    per_step = total_pages // n_steps
    ok = per_step * n_steps == total_pages and per_step % ppb == 0 and 0 < per_step <= 2 * SUBLANES
    return per_step if ok else 0


def _mlp(h2, x1, w_up, w_down, w_post, page_job=None):
    t, d = x1.shape
    d_ff = w_up.shape[1]
    tm, tf = _mlp_tiles(t, d_ff)
    nf = d_ff // tf
    n_steps = (t // tm) * nf
    in_specs = [
        pl.BlockSpec((tm, d), lambda i, f, pt: (i, 0)),
        pl.BlockSpec((tm, d), lambda i, f, pt: (i, 0)),
        pl.BlockSpec((d, tf), lambda i, f, pt: (0, f)),
        pl.BlockSpec((tf, d), lambda i, f, pt: (f, 0)),
        pl.BlockSpec((1, d), lambda i, f, pt: (0, 0)),
    ]
    out_specs = [pl.BlockSpec((tm, d), lambda i, f, pt: (i, 0))]
    out_shape = [jax.ShapeDtypeStruct((t, d), F32)]
    args = [h2, x1, w_up, w_down, w_post]
    n_pages, ppb, pt = 0, 1, jnp.zeros((1,), jnp.int32)
    if page_job is not None:
        cache_k, pt, layer = page_job
        page = cache_k.shape[2]
        ppb = MOBA_BLOCK // page
        n_pages = _mlp_pages_per_step(t, d_ff, pt.shape[0], ppb)
        assert n_pages > 0
        in_specs += [pl.BlockSpec((None, None, page, N_HEADS, HEAD_DIM),
                                  lambda i, f, pt, r=r: (layer, pt[(i * nf + f) * n_pages + r], 0, 0, 0))
                     for r in range(n_pages)]
        args += [cache_k] * n_pages
        out_specs.append(pl.BlockSpec((None, n_pages // ppb, N_HEADS, HEAD_DIM), lambda i, f, pt: (i * nf + f, 0, 0, 0)))
        out_shape.append(jax.ShapeDtypeStruct((n_steps, n_pages // ppb, N_HEADS, HEAD_DIM), F32))
    res = pl.pallas_call(
        functools.partial(_mlp_body, n_pages=n_pages, ppb=ppb),
        grid_spec=pltpu.PrefetchScalarGridSpec(
            num_scalar_prefetch=1,
            grid=(t // tm, nf),
            in_specs=in_specs,
            out_specs=out_specs,
        ),
        out_shape=out_shape,
        compiler_params=_params("arbitrary", "arbitrary"),
        name="mlp",
    )(pt, *args)
    return res if page_job is not None else res[0]


def _rope_tables(pos, rows, scale):
    half = HEAD_DIM // 2
    inv_freq = 1.0 / (ROPE_THETA ** (jnp.arange(half, dtype=F32) / half))
    ang = pos.astype(F32)[:, None] * inv_freq[None, :]
    cos = jnp.concatenate([jnp.cos(ang), jnp.cos(ang)], axis=1) * scale
    sin = jnp.concatenate([-jnp.sin(ang), jnp.sin(ang)], axis=1) * scale
    reps = max(1, rows // pos.shape[0])
    return jnp.tile(cos, (reps, 1)), jnp.tile(sin, (reps, 1))


def _layer(x, pos, conv_buf, dn_state, moba_fn, lw, *, q_dtype, page_job=None):
    (w_pre_mix, w_in, w_zba, conv_w, a_log, dt_bias, dn_norm_w, w_out, w_post_mix, w_pre_mlp,
     w_up, w_down, w_post_mlp) = lw
    w = WIDTH
    batch, seq, d = x.shape
    t = batch * seq
    x2 = x.reshape(t, d)
    tm = _row_tile(t, 1024)
    assert tm % seq == 0 or seq % tm == 0
    q, h = _proj(x2, w_in, cols=(0, w), out_dtype=q_dtype, rope=_rope_tables(pos, tm, ATTN_SCALE),
                 norm_w=w_pre_mix.reshape(1, d))
    k = _proj(h, w_in, cols=(w, 2 * w), out_dtype=F32, rope=_rope_tables(pos, tm, 1.0))
    v = _proj(h, w_in, cols=(2 * w, 3 * w), out_dtype=F32)
    qkv_d = _proj(h, w_in, cols=(3 * w, 6 * w), out_dtype=F32)
    zba = _proj(h, w_zba, out_dtype=F32)
    o_a = moba_fn(q, k, v)

    c = DN_CHUNK
    seq_pad = -(-seq // c) * c
    if seq_pad == seq:
        dn_in, n_valid = (qkv_d, zba), c
    else:
        assert seq < c
        padrows = lambda a: jnp.pad(a.reshape(batch, seq, -1), ((0, 0), (0, c - seq), (0, 0))).reshape(batch * c, -1)
        dn_in, n_valid = (padrows(qkv_d), padrows(zba)), seq
    o_d, new_state = _deltanet(*dn_in, conv_buf, dn_state, conv_w, a_log, dt_bias, dn_norm_w, batch, seq_pad, n_valid,
                               rows=math.gcd(batch, DN_ROWS))
    if seq_pad != seq:
        o_d = o_d.reshape(batch, c, WIDTH)[:, :seq].reshape(t, WIDTH)
    keep = CONV_W - 1
    tail = qkv_d.reshape(batch, seq, -1)[:, max(0, seq - keep):]
    new_buf = tail if seq >= keep else jnp.concatenate([conv_buf, tail], axis=1)[:, -keep:]

    x1, h2 = _out_proj(o_a, o_d, x2, w_out, w_post_mix.reshape(1, d), w_pre_mlp.reshape(1, d))
    kmean = None
    if page_job is not None and _mlp_pages_per_step(t, w_up.shape[1], page_job[1].shape[0],
                                                    MOBA_BLOCK // page_job[0].shape[2]):
        y, kmean = _mlp(h2, x1, w_up, w_down, w_post_mlp.reshape(1, d), page_job)
    else:
        y = _mlp(h2, x1, w_up, w_down, w_post_mlp.reshape(1, d))
    heads = (batch, seq, N_HEADS, HEAD_DIM)
    return y.reshape(batch, seq, d), k.reshape(heads), v.reshape(heads), new_buf, new_state, kmean


def kernel(x_prompt, x_sample, cache_k, cache_v, state_dn, state_conv, page_table, w_pre_mix, w_in, conv_w,
           a_log, dt_bias, dn_norm_w, w_out, w_post_mix, w_pre_mlp, w_up, w_down, w_post_mlp):
    depth = w_in.shape[0]
    batch, seq, _ = x_prompt.shape
    dec_batch, dec_seq, _ = x_sample.shape
    past = page_table.shape[1] * cache_k.shape[2]
    pos_p = jnp.arange(seq, dtype=jnp.int32)
    pos_s = past + jnp.arange(dec_seq, dtype=jnp.int32)
    hp, hs = x_prompt, x_sample
    outs = [[] for _ in range(8)]
    for l in range(depth):
        w_in_b = w_in[l].astype(BF16)
        w_zba = jnp.pad(w_in_b[:, 6 * WIDTH:], ((0, 0), (0, LANES - 2 * N_HEADS)))
        lw = (w_pre_mix[l], w_in_b, w_zba, conv_w[l], a_log[l], dt_bias[l], dn_norm_w[l],
              w_out[l].astype(BF16), w_post_mix[l], w_pre_mlp[l], w_up[l].astype(BF16), w_down[l].astype(BF16),
              w_post_mlp[l])
        zero_buf = jnp.zeros((batch, CONV_W - 1, 3 * WIDTH), x_prompt.dtype)
        zero_state = jnp.zeros((batch, N_HEADS, HEAD_DIM, HEAD_DIM), state_dn.dtype)
        prompt_moba = functools.partial(_moba_prompt, batch=batch, seq=seq)
        hp, kp, vp, cvp, dnp, kmean = _layer(hp, pos_p, zero_buf, zero_state, prompt_moba, lw, q_dtype=BF16,
                                             page_job=(cache_k, page_table.reshape(-1), l))
        if kmean is not None:
            kmean = kmean.reshape(dec_batch, -1, N_HEADS, HEAD_DIM)
        sample_moba = functools.partial(_moba_sample, cache_k=cache_k, cache_v=cache_v, page_table=page_table,
                                        layer=l, dec_batch=dec_batch, dec_seq=dec_seq, kmean=kmean)
        hs, kss, vss, cvs, dns, _ = _layer(hs, pos_s, state_conv[l], state_dn[l], sample_moba, lw, q_dtype=F32)
        for lst, val in zip(outs, (kp, vp, kss, vss, dnp, dns, cvp, cvs)):
            lst.append(val)
    return (hp, hs) + tuple(jnp.stack(o) for o in outs)
```
